```python
import math
import jax, jax.numpy as jnp
from jax import lax
import numpy as np

D_MODEL = 1024
BATCH = 16
SEQ = 4096
DEPTH = 1

H_A = 8
KV_A = 2
REP_A = H_A // KV_A
HD_A = 64
D_A = H_A * HD_A
IDX_H = 8
IDX_D = 64
TOPK_MAX = 256
H_B = 8
D_NOPE = 64
D_ROPE = 32
V_B = 64
Q_LORA = 384
KV_LORA = 256
D_B = H_B * V_B
D_MIX = D_A + D_B
ROPE_THETA = 10000.0
Q_BLOCK = 128
DEEPNORM_ALPHA = (2.0 * DEPTH) ** 0.25
DEEPNORM_BETA = (8.0 * DEPTH) ** -0.25
NEG_INF = -1e30

IN_SIZES = (
    D_A,
    KV_A * HD_A,
    KV_A * HD_A,
    IDX_H * IDX_D,
    IDX_D,
    IDX_H,
    Q_LORA,
    KV_LORA,
    D_ROPE,
    D_MIX,
)
D_IN = sum(IN_SIZES)

kernel_name = "hybrid_dsa_mla_gated_deepnorm"


def _rms(x, g, eps=1e-6):
    xf = x.astype(jnp.float32)
    y = xf * lax.rsqrt(jnp.mean(xf * xf, axis=-1, keepdims=True) + eps)
    return y.astype(x.dtype) * g


def _layernorm(x, g, b, eps=1e-5):
    xf = x.astype(jnp.float32)
    mu = jnp.mean(xf, axis=-1, keepdims=True)
    var = jnp.mean(jnp.square(xf - mu), axis=-1, keepdims=True)
    return ((xf - mu) * lax.rsqrt(var + eps)).astype(x.dtype) * g + b


def _rope(x, cos, sin):
    half = x.shape[-1] // 2
    x1, x2 = x[..., :half], x[..., half:]
    return jnp.concatenate([x1 * cos - x2 * sin, x1 * sin + x2 * cos], axis=-1).astype(x.dtype)


def _hybrid_layer(x, positions, w_in, g_q_lat, w_uq, g_kv_lat, w_ukv, g_out_a, g_out_b, w_out, ln_g, ln_b):
    B, S, _ = x.shape
    f32 = jnp.float32
    k_top = min(TOPK_MAX, S // 4)
    nb = S // Q_BLOCK

    proj = x @ w_in
    splits = [int(v) for v in np.cumsum(IN_SIZES)[:-1]]
    qa, ka, va, iq, ik, iw, cq, ckv, kr, z = jnp.split(proj, splits, axis=-1)

    q_a = qa.reshape(B, S, KV_A, REP_A, HD_A)
    k_a = ka.reshape(B, S, KV_A, HD_A)
    v_a = va.reshape(B, S, KV_A, HD_A)
    iq = iq.reshape(B, S, IDX_H, IDX_D)
    iw = iw * (IDX_H ** -0.5 * IDX_D ** -0.5)
    slopes = jnp.exp2(-8.0 * jnp.arange(1, H_A + 1, dtype=f32) / H_A).reshape(KV_A, REP_A)

    inv_freq = ROPE_THETA ** (-jnp.arange(0, D_ROPE, 2, dtype=f32) / D_ROPE)
    ang = positions.astype(f32)[..., None] * inv_freq
    cos, sin = jnp.cos(ang), jnp.sin(ang)
    q_b = (_rms(cq, g_q_lat) @ w_uq).reshape(B, S, H_B, D_NOPE + D_ROPE)
    q_nope = q_b[..., :D_NOPE]
    q_rope = _rope(q_b[..., D_NOPE:], cos[:, :, None, :], sin[:, :, None, :])
    kv_b = (_rms(ckv, g_kv_lat) @ w_ukv).reshape(B, S, H_B, D_NOPE + V_B)
    k_nope, v_b = kv_b[..., :D_NOPE], kv_b[..., D_NOPE:]
    k_rope = _rope(kr, cos, sin)
    scale_b = (D_NOPE + D_ROPE) ** -0.5
    s_idx = jnp.arange(S)

    def block(i):
        q0 = i * Q_BLOCK
        sl = lambda a: lax.dynamic_slice_in_dim(a, q0, Q_BLOCK, axis=1)
        t_idx = q0 + jnp.arange(Q_BLOCK)
        causal = s_idx[None, :] <= t_idx[:, None]

        rel = jax.nn.relu(jnp.einsum('bthd,bsd->bths', sl(iq), ik).astype(f32))
        isc = jnp.einsum('bths,bth->bts', rel, sl(iw).astype(f32))
        isc = jnp.where(causal[None], isc, NEG_INF)
        _, sel = lax.top_k(isc, k_top)
        valid = sel <= t_idx[None, :, None]
        k_g = jax.vmap(lambda a, idx: a[idx])(k_a, sel)
        v_g = jax.vmap(lambda a, idx: a[idx])(v_a, sel)
        pos_sel = jnp.take_along_axis(positions, sel.reshape(B, -1), axis=1).reshape(B, Q_BLOCK, k_top)
        dist = (sl(positions)[:, :, None] - pos_sel).astype(f32)
        sa = jnp.einsum('btgrd,btkgd->btgrk', sl(q_a), k_g).astype(f32) * (HD_A ** -0.5)
        sa = sa - slopes[None, None, :, :, None] * jnp.abs(dist)[:, :, None, None, :]
        sa = jnp.where(valid[:, :, None, None, :], sa, NEG_INF)
        pa = jax.nn.softmax(sa, axis=-1).astype(v_g.dtype)
        o_a = jnp.einsum('btgrk,btkgd->btgrd', pa, v_g).reshape(B, Q_BLOCK, D_A)

        sb = (jnp.einsum('bthd,bshd->bhts', sl(q_nope), k_nope)
              + jnp.einsum('bthd,bsd->bhts', sl(q_rope), k_rope)).astype(f32) * scale_b
        sb = jnp.where(causal[None, None], sb, NEG_INF)
        pb = jax.nn.softmax(sb, axis=-1).astype(v_b.dtype)
        o_b = jnp.einsum('bhts,bshd->bthd', pb, v_b).reshape(B, Q_BLOCK, D_B)
        return o_a, o_b

    o_a, o_b = lax.map(block, jnp.arange(nb))
    o_a = o_a.transpose(1, 0, 2, 3).reshape(B, S, D_A)
    o_b = o_b.transpose(1, 0, 2, 3).reshape(B, S, D_B)

    y = jnp.concatenate([_rms(o_a, g_out_a), _rms(o_b, g_out_b)], axis=-1) * jax.nn.silu(z)
    out = y @ w_out
    return _layernorm(DEEPNORM_ALPHA * x + out, ln_g, ln_b)


def setup_inputs(seed: int = 0) -> dict:
    key = jax.random.key(seed)
    ks = jax.random.split(key, 12)
    f32 = jnp.float32
    x = jax.random.normal(ks[0], (BATCH, SEQ, D_MODEL), f32)
    offsets = jax.random.randint(ks[1], (BATCH, 1), 0, 1024, dtype=jnp.int32)
    positions = offsets + jnp.arange(SEQ, dtype=jnp.int32)[None, :]
    off_v = IN_SIZES[0] + IN_SIZES[1]
    col_scale = jnp.ones((D_IN,), f32).at[off_v:off_v + IN_SIZES[2]].set(DEEPNORM_BETA)
    w_in = jax.random.normal(ks[2], (DEPTH, D_MODEL, D_IN), f32) * (D_MODEL ** -0.5) * col_scale
    g_q_lat = 1.0 + 0.01 * jax.random.normal(ks[3], (DEPTH, Q_LORA), f32)
    w_uq = jax.random.normal(ks[4], (DEPTH, Q_LORA, H_B * (D_NOPE + D_ROPE)), f32) * (Q_LORA ** -0.5)
    g_kv_lat = 1.0 + 0.01 * jax.random.normal(ks[5], (DEPTH, KV_LORA), f32)
    w_ukv = jax.random.normal(ks[6], (DEPTH, KV_LORA, H_B, D_NOPE + V_B), f32) * (KV_LORA ** -0.5)
    w_ukv = w_ukv.at[..., D_NOPE:].multiply(DEEPNORM_BETA).reshape(DEPTH, KV_LORA, H_B * (D_NOPE + V_B))
    g_out_a = 1.0 + 0.01 * jax.random.normal(ks[7], (DEPTH, D_A), f32)
    g_out_b = 1.0 + 0.01 * jax.random.normal(ks[8], (DEPTH, D_B), f32)
    w_out = jax.random.normal(ks[9], (DEPTH, D_MIX, D_MODEL), f32) * (D_MIX ** -0.5) * DEEPNORM_BETA
    ln_g = 1.0 + 0.01 * jax.random.normal(ks[10], (DEPTH, D_MODEL), f32)
    ln_b = 0.01 * jax.random.normal(ks[11], (DEPTH, D_MODEL), f32)
    return {"x": x, "positions": positions, "w_in": w_in, "g_q_lat": g_q_lat, "w_uq": w_uq,
            "g_kv_lat": g_kv_lat, "w_ukv": w_ukv, "g_out_a": g_out_a, "g_out_b": g_out_b,
            "w_out": w_out, "ln_g": ln_g, "ln_b": ln_b}


def reference(x, positions, w_in, g_q_lat, w_uq, g_kv_lat, w_ukv, g_out_a, g_out_b, w_out, ln_g, ln_b):
    h = x
    for l in range(DEPTH):
        h = _hybrid_layer(h, positions, w_in[l], g_q_lat[l], w_uq[l], g_kv_lat[l], w_ukv[l],
                          g_out_a[l], g_out_b[l], w_out[l], ln_g[l], ln_b[l])
    return h
```

```python
import functools

import numpy as np
import jax
import jax.numpy as jnp
from jax import lax
from jax.experimental import pallas as pl
from jax.experimental.pallas import tpu as pltpu

D_MODEL = 1024
H_A, KV_A, HD_A = 8, 2, 64
REP_A = H_A // KV_A
D_A = H_A * HD_A
IDX_H, IDX_D = 8, 64
TOPK_MAX = 256
H_B, D_NOPE, D_ROPE, V_B = 8, 64, 32, 64
Q_LORA, KV_LORA = 384, 256
D_B = H_B * V_B
D_MIX = D_A + D_B
ROPE_THETA = 10000.0
NEG_INF = -1e30
IN_SIZES = (D_A, KV_A * HD_A, KV_A * HD_A, IDX_H * IDX_D, IDX_D, IDX_H, Q_LORA, KV_LORA, D_ROPE, D_MIX)
IW_SCALE = IDX_H ** -0.5 * IDX_D ** -0.5
SCALE_B = (D_NOPE + D_ROPE) ** -0.5
QB_PAD = 128
SLOPES = tuple(2.0 ** (-8.0 * (h + 1) / H_A) for h in range(H_A))

TM = 512
TQ = 256
TK = 256
BISECT_FIXED = 22
BISECT_CAP = 320
VMEM_LIMIT = 48 * 1024 * 1024

_NT = (((1,), (1,)), ((), ()))
_TN = (((0,), (0,)), ((), ()))


def _rms_rows(v, g, eps=1e-6):
    return v * lax.rsqrt(jnp.mean(v * v, axis=-1, keepdims=True) + eps) * g


def _proj_kernel(x_ref, posr_ref, posc_ref, wqaT, wiqT, wzT, wvaT, wiwT, wka, wik, wcq, wckv, wkrp, wkrr,
                 gq, wuqT, gkv, wukp, wuvT, invr, invc,
                 qaT_o, ka_o, vaT_o, iqT_o, ik_o, iwT_o, qbT_o, kcat_o, vbT_o, szT_o):
    f32, bf = jnp.float32, jnp.bfloat16
    xb = x_ref[0].astype(bf)

    def nt(w):
        return lax.dot_general(w, xb, _NT, preferred_element_type=f32)

    def nn(w):
        return jnp.dot(xb, w, preferred_element_type=f32)

    qaT_o[0] = nt(wqaT[...]).astype(bf)
    iqT_o[0] = nt(wiqT[...]).astype(bf)
    iwT_o[0] = nt(wiwT[...]) * IW_SCALE
    va = nt(wvaT[...]).astype(bf)
    for c in range(TM // TK):
        vaT_o[0, c] = va[:, c * TK:(c + 1) * TK]
    z = nt(wzT[...])
    szT_o[0] = (z * jax.nn.sigmoid(z)).astype(bf)
    kk = nn(wka[...])
    for g in range(KV_A):
        ka_o[0, g] = kk[:, g * HD_A:(g + 1) * HD_A].astype(bf)
    ik_o[0] = nn(wik[...]).astype(bf)

    cqn = _rms_rows(nn(wcq[...]), gq[...]).astype(bf)
    qb = lax.dot_general(wuqT[...], cqn, _NT, preferred_element_type=f32)
    ang_t = invc[...] * posr_ref[0]
    cos_t, sin_t = jnp.cos(ang_t), jnp.sin(ang_t)
    half = D_ROPE // 2
    for h in range(H_B):
        b, o = h * (D_NOPE + D_ROPE), h * QB_PAD
        qbT_o[0, o:o + D_NOPE] = (qb[b:b + D_NOPE] * SCALE_B).astype(bf)
        x1 = qb[b + D_NOPE:b + D_NOPE + half]
        x2 = qb[b + D_NOPE + half:b + D_NOPE + D_ROPE]
        qbT_o[0, o + D_NOPE:o + D_NOPE + half] = ((x1 * cos_t - x2 * sin_t) * SCALE_B).astype(bf)
        qbT_o[0, o + D_NOPE + half:o + D_NOPE + D_ROPE] = ((x1 * sin_t + x2 * cos_t) * SCALE_B).astype(bf)
        qbT_o[0, o + D_NOPE + D_ROPE:o + QB_PAD] = jnp.zeros((QB_PAD - D_NOPE - D_ROPE, TM), bf)

    ckvn = _rms_rows(nn(wckv[...]), gkv[...]).astype(bf)
    knp = jnp.dot(ckvn, wukp[...], preferred_element_type=f32)
    ang = posc_ref[0] * invr[...]
    kro = nn(wkrp[...]) * jnp.cos(ang) + nn(wkrr[...]) * jnp.sin(ang)
    for h in range(H_B):
        kcat_o[0, h] = (knp[:, h * QB_PAD:(h + 1) * QB_PAD] + kro).astype(bf)
    vb = lax.dot_general(wuvT[...], ckvn, _NT, preferred_element_type=f32).astype(bf)
    for c in range(TM // TK):
        vbT_o[0, c] = vb[:, c * TK:(c + 1) * TK]


def _attn_a_kernel(iqT_ref, iwT_ref, ik_ref, qaT_ref, ka_ref, vaT_ref, posr_ref, posc_ref, o_ref,
                   isc_scr, bias_scr, st_scr, m_scr, l_scr, acc_scr, flag_smem, *, ktop):
    f32 = jnp.float32
    qi = pl.program_id(1)
    t_idx = qi * TQ + lax.broadcasted_iota(jnp.int32, (1, TQ), 1)
    iw = iwT_ref[0]

    def rows_of(j):
        return pl.ds(pl.multiple_of(j * TK, TK), TK)

    def idx_tile(j):
        k = ik_ref[0, rows_of(j), :]
        tot = jnp.zeros((TK, TQ), f32)
        for h in range(IDX_H):
            rel = jnp.dot(k, iqT_ref[0, h * IDX_D:(h + 1) * IDX_D, :], preferred_element_type=f32)
            tot = tot + jnp.maximum(rel, 0.0) * iw[h:h + 1, :]
        return tot

    def p1(j, amax):
        tot = idx_tile(j)
        isc_scr[rows_of(j), :] = tot
        return jnp.maximum(amax, jnp.max(jnp.abs(tot), axis=0, keepdims=True))

    amax = lax.fori_loop(0, qi, p1, jnp.zeros((1, TQ), f32))
    tot = idx_tile(qi)
    s_idx = qi * TK + lax.broadcasted_iota(jnp.int32, (TK, TQ), 0)
    causal = s_idx <= t_idx
    isc_scr[rows_of(qi), :] = jnp.where(causal, tot, NEG_INF)
    amax = jnp.maximum(amax, jnp.max(jnp.where(causal, jnp.abs(tot), 0.0), axis=0, keepdims=True))

    want = jnp.minimum(t_idx + 1, ktop).astype(f32)
    n_tiles = qi + 1

    def count_ge(mid):
        def body(j, acc8):
            ind = jnp.where(isc_scr[rows_of(j), :] >= mid, 1.0, 0.0)
            return acc8 + ind.reshape(TK // 8, 8, TQ).sum(axis=0)
        acc8 = lax.fori_loop(0, n_tiles, body, jnp.zeros((8, TQ), f32))
        return jnp.sum(acc8, axis=0, keepdims=True)

    def bisect(_, st):
        lo, hi, clo, chi = st
        mid = 0.5 * (lo + hi)
        c = count_ge(mid)
        ge = c >= want
        return (jnp.where(ge, mid, lo), jnp.where(ge, hi, mid), jnp.where(ge, c, clo), jnp.where(ge, chi, c))

    st0 = (-amax, amax * 1.0001 + 1e-30, (t_idx + 1).astype(f32), jnp.zeros((1, TQ), f32))
    lo, hi, clo, chi = lax.fori_loop(0, BISECT_FIXED, bisect, st0)
    st_scr[0:1, :] = lo
    st_scr[1:2, :] = hi
    st_scr[2:3, :] = clo
    st_scr[3:4, :] = chi

    def unresolved_any(clo, chi, distinct):
        u = jnp.where((clo - chi) > (want - chi), distinct, 0.0)
        return (jnp.max(u) > 0.5).astype(jnp.int32)

    def refine_cond(c):
        it, more = c
        return jnp.logical_and(more > 0, it < BISECT_CAP)

    def refine(c):
        it, _ = c
        lo, hi, clo, chi = st_scr[0:1, :], st_scr[1:2, :], st_scr[2:3, :], st_scr[3:4, :]
        mid = 0.5 * (lo + hi)

        def body(j, carry):
            a8, mn8, mx8 = carry
            x = isc_scr[rows_of(j), :]
            ind = jnp.where(x >= mid, 1.0, 0.0)
            inside = jnp.where(x >= lo, jnp.where(x < hi, 1.0, 0.0), 0.0) > 0.5
            mn = jnp.where(inside, x, 3e38).reshape(TK // 8, 8, TQ).min(axis=0)
            mx = jnp.where(inside, x, -3e38).reshape(TK // 8, 8, TQ).max(axis=0)
            return (a8 + ind.reshape(TK // 8, 8, TQ).sum(axis=0), jnp.minimum(mn8, mn), jnp.maximum(mx8, mx))

        a8, mn8, mx8 = lax.fori_loop(
            0, n_tiles, body,
            (jnp.zeros((8, TQ), f32), jnp.full((8, TQ), 3e38, f32), jnp.full((8, TQ), -3e38, f32)))
        cnt = jnp.sum(a8, axis=0, keepdims=True)
        mn = jnp.min(mn8, axis=0, keepdims=True)
        mx = jnp.max(mx8, axis=0, keepdims=True)
        ge = cnt >= want
        lo2, hi2 = jnp.where(ge, mid, lo), jnp.where(ge, hi, mid)
        clo2, chi2 = jnp.where(ge, cnt, clo), jnp.where(ge, chi, cnt)
        st_scr[0:1, :] = lo2
        st_scr[1:2, :] = hi2
        st_scr[2:3, :] = clo2
        st_scr[3:4, :] = chi2
        return it + 1, unresolved_any(clo2, chi2, jnp.where(mx > mn, 1.0, 0.0))

    lax.while_loop(refine_cond, refine,
                   (jnp.int32(BISECT_FIXED), unresolved_any(clo, chi, jnp.ones((1, TQ), f32))))
    lo, hi, clo, chi = st_scr[0:1, :], st_scr[1:2, :], st_scr[2:3, :], st_scr[3:4, :]
    need = want - chi
    flag_smem[0] = unresolved_any(clo, chi, jnp.ones((1, TQ), f32))
    st_scr[4:5, :] = need
    st_scr[5:6, :] = jnp.zeros((1, TQ), f32)

    m_scr[...] = jnp.full(m_scr.shape, -3e38, f32)
    l_scr[...] = jnp.zeros(l_scr.shape, f32)
    acc_scr[...] = jnp.zeros(acc_scr.shape, f32)
    pq = posr_ref[0]

    def p3(j, _):
        rows = rows_of(j)
        x = isc_scr[rows, :]
        nad = -jnp.abs(pq - posc_ref[0, rows, :])
        lo = st_scr[0:1, :]

        @pl.when(flag_smem[0] == 0)
        def _():
            bias_scr[...] = jnp.where(x >= lo, nad, NEG_INF)

        @pl.when(flag_smem[0] != 0)
        def _():
            hi, need, seen = st_scr[1:2, :], st_scr[4:5, :], st_scr[5:6, :]
            above = jnp.where(x >= hi, 1.0, 0.0)
            tied = jnp.where(x >= lo, 1.0, 0.0) - above
            r_i = lax.broadcasted_iota(jnp.int32, (TK, TK), 0)
            c_i = lax.broadcasted_iota(jnp.int32, (TK, TK), 1)
            tri = jnp.where(r_i >= c_i, 1.0, 0.0).astype(jnp.bfloat16)
            incl = jnp.dot(tri, tied.astype(jnp.bfloat16), preferred_element_type=f32)
            rank = incl - tied + seen
            keep = above + tied * jnp.where(rank < need, 1.0, 0.0)
            bias_scr[...] = jnp.where(keep > 0.5, nad, NEG_INF)
            st_scr[5:6, :] = seen + incl[TK - 1:TK, :]

        nd = bias_scr[...]
        for h in range(H_A):
            g = h // REP_A
            k = ka_ref[0, g, rows, :]
            s = jnp.dot(k, qaT_ref[0, h * HD_A:(h + 1) * HD_A, :], preferred_element_type=f32) + SLOPES[h] * nd
            m_old = m_scr[h:h + 1, :]
            m_new = jnp.maximum(m_old, jnp.max(s, axis=0, keepdims=True))
            p = jnp.exp(s - m_new)
            alpha = jnp.exp(m_old - m_new)
            m_scr[h:h + 1, :] = m_new
            l_scr[h:h + 1, :] = alpha * l_scr[h:h + 1, :] + jnp.sum(p, axis=0, keepdims=True)
            v_t = vaT_ref[0, j, g * HD_A:(g + 1) * HD_A, :]
            acc_scr[h] = alpha * acc_scr[h] + jnp.dot(v_t, p.astype(jnp.bfloat16), preferred_element_type=f32)
        return 0

    lax.fori_loop(0, n_tiles, p3, 0)
    for h in range(H_A):
        o_ref[0, h * HD_A:(h + 1) * HD_A, :] = acc_scr[h] / l_scr[h:h + 1, :]


def _attn_b_kernel(qbT_ref, kcat_ref, vbT_ref, o_ref, m_scr, l_scr, acc_scr):
    f32 = jnp.float32
    qi = pl.program_id(1)
    m_scr[...] = jnp.full(m_scr.shape, -3e38, f32)
    l_scr[...] = jnp.zeros(l_scr.shape, f32)
    acc_scr[...] = jnp.zeros(acc_scr.shape, f32)

    def tile(j, masked):
        rows = pl.ds(pl.multiple_of(j * TK, TK), TK)
        if masked:
            s_idx = j * TK + lax.broadcasted_iota(jnp.int32, (TK, TQ), 0)
            t_idx = qi * TQ + lax.broadcasted_iota(jnp.int32, (TK, TQ), 1)
            causal = s_idx <= t_idx
        for h in range(H_B):
            k = kcat_ref[0, h, rows, :]
            s = jnp.dot(k, qbT_ref[0, h * QB_PAD:(h + 1) * QB_PAD, :], preferred_element_type=f32)
            if masked:
                s = jnp.where(causal, s, NEG_INF)
            m_old = m_scr[h:h + 1, :]
            m_new = jnp.maximum(m_old, jnp.max(s, axis=0, keepdims=True))
            p = jnp.exp(s - m_new)
            alpha = jnp.exp(m_old - m_new)
            m_scr[h:h + 1, :] = m_new
            l_scr[h:h + 1, :] = alpha * l_scr[h:h + 1, :] + jnp.sum(p, axis=0, keepdims=True)
            v_t = vbT_ref[0, j, h * V_B:(h + 1) * V_B, :]
            acc_scr[h] = alpha * acc_scr[h] + jnp.dot(v_t, p.astype(jnp.bfloat16), preferred_element_type=f32)

    def body(j, _):
        tile(j, False)
        return 0

    lax.fori_loop(0, qi, body, 0)
    tile(qi, True)
    for h in range(H_B):
        o_ref[0, h * V_B:(h + 1) * V_B, :] = acc_scr[h] / l_scr[h:h + 1, :]


def _out_kernel(oaT_ref, obT_ref, szT_ref, x_ref, wout_ref, ga_ref, gb_ref, lng_ref, lnb_ref, o_ref, *, alpha):
    f32 = jnp.float32

    def rms_cols(v, g, eps=1e-6):
        return v * lax.rsqrt(jnp.mean(v * v, axis=0, keepdims=True) + eps) * g

    y_t = jnp.concatenate([rms_cols(oaT_ref[0], ga_ref[...]), rms_cols(obT_ref[0], gb_ref[...])], axis=0)
    y_t = (y_t * szT_ref[0].astype(f32)).astype(jnp.bfloat16)
    out = lax.dot_general(y_t, wout_ref[...], _TN, preferred_element_type=f32)
    hres = alpha * x_ref[0] + out
    mu = jnp.mean(hres, axis=-1, keepdims=True)
    d = hres - mu
    var = jnp.mean(d * d, axis=-1, keepdims=True)
    o_ref[0] = d * lax.rsqrt(var + 1e-5) * lng_ref[...] + lnb_ref[...]


def _full(shape):
    return pl.BlockSpec(shape, lambda *_: (0,) * len(shape))


def _layer(x, posr, posc, w_in, g_q_lat, w_uq, g_kv_lat, w_ukv, g_out_a, g_out_b, w_out, ln_g, ln_b, alpha):
    B, S, _ = x.shape
    f32, bf = jnp.float32, jnp.bfloat16
    assert S % TM == 0 and TM % TK == 0 and TQ == TK
    ktop = min(TOPK_MAX, S // 4)
    n_kt = S // TK

    offs = np.concatenate([[0], np.cumsum(IN_SIZES)])
    wqa, wka, wva, wiq, wik, wiw, wcq, wckv, wkr, wz = [w_in[:, offs[i]:offs[i + 1]] for i in range(10)]
    half = D_ROPE // 2
    pad = jnp.zeros((D_MODEL, QB_PAD), f32)
    wkrp = pad.at[:, D_NOPE:D_NOPE + D_ROPE].set(wkr)
    wkrr = pad.at[:, D_NOPE:D_NOPE + half].set(-wkr[:, half:]).at[:, D_NOPE + half:D_NOPE + D_ROPE].set(wkr[:, :half])
    wukv3 = w_ukv.reshape(KV_LORA, H_B, D_NOPE + V_B)
    wukp = jnp.zeros((KV_LORA, H_B, QB_PAD), f32).at[:, :, :D_NOPE].set(wukv3[:, :, :D_NOPE]).reshape(KV_LORA, H_B * QB_PAD)
    wuvT = wukv3[:, :, D_NOPE:].reshape(KV_LORA, D_B).T
    inv_freq = ROPE_THETA ** (-jnp.arange(0, D_ROPE, 2, dtype=f32) / D_ROPE)
    invr = jnp.zeros((1, QB_PAD), f32).at[0, D_NOPE:D_NOPE + half].set(inv_freq).at[0, D_NOPE + half:D_NOPE + D_ROPE].set(inv_freq)
    invc = inv_freq.reshape(half, 1)
    weights = [
        (wqa * HD_A ** -0.5).T.astype(bf), wiq.T.astype(bf), wz.T.astype(bf), wva.T.astype(bf), wiw.T.astype(bf),
        wka.astype(bf), wik.astype(bf), wcq.astype(bf), wckv.astype(bf), wkrp.astype(bf), wkrr.astype(bf),
        g_q_lat.reshape(1, Q_LORA), w_uq.T.astype(bf), g_kv_lat.reshape(1, KV_LORA), wukp.astype(bf), wuvT.astype(bf),
        invr, invc,
    ]

    grid_m = (B, S // TM)
    tok = lambda b, i: (b, i, 0)
    col = lambda b, i: (b, 0, i)
    proj_out_shapes = [
        jax.ShapeDtypeStruct((B, D_A, S), bf),
        jax.ShapeDtypeStruct((B, KV_A, S, HD_A), bf),
        jax.ShapeDtypeStruct((B, n_kt, KV_A * HD_A, TK), bf),
        jax.ShapeDtypeStruct((B, IDX_H * IDX_D, S), bf),
        jax.ShapeDtypeStruct((B, S, IDX_D), bf),
        jax.ShapeDtypeStruct((B, IDX_H, S), f32),
        jax.ShapeDtypeStruct((B, H_B * QB_PAD, S), bf),
        jax.ShapeDtypeStruct((B, H_B, S, QB_PAD), bf),
        jax.ShapeDtypeStruct((B, n_kt, D_B, TK), bf),
        jax.ShapeDtypeStruct((B, D_MIX, S), bf),
    ]
    proj_out_specs = [
        pl.BlockSpec((1, D_A, TM), col),
        pl.BlockSpec((1, KV_A, TM, HD_A), lambda b, i: (b, 0, i, 0)),
        pl.BlockSpec((1, TM // TK, KV_A * HD_A, TK), lambda b, i: (b, i, 0, 0)),
        pl.BlockSpec((1, IDX_H * IDX_D, TM), col),
        pl.BlockSpec((1, TM, IDX_D), tok),
        pl.BlockSpec((1, IDX_H, TM), col),
        pl.BlockSpec((1, H_B * QB_PAD, TM), col),
        pl.BlockSpec((1, H_B, TM, QB_PAD), lambda b, i: (b, 0, i, 0)),
        pl.BlockSpec((1, TM // TK, D_B, TK), lambda b, i: (b, i, 0, 0)),
        pl.BlockSpec((1, D_MIX, TM), col),
    ]
    params = pltpu.CompilerParams(dimension_semantics=("parallel", "arbitrary"), vmem_limit_bytes=VMEM_LIMIT)
    qaT, ka, vaT, iqT, ik, iwT, qbT, kcat, vbT, szT = pl.pallas_call(
        _proj_kernel,
        grid=grid_m,
        in_specs=[pl.BlockSpec((1, TM, D_MODEL), tok), pl.BlockSpec((1, 1, TM), col), pl.BlockSpec((1, TM, 1), tok)]
        + [_full(w.shape) for w in weights],
        out_specs=proj_out_specs,
        out_shape=proj_out_shapes,
        compiler_params=params,
        name="proj",
    )(x, posr, posc, *weights)

    grid_q = (B, S // TQ)
    oaT = pl.pallas_call(
        functools.partial(_attn_a_kernel, ktop=ktop),
        grid=grid_q,
        in_specs=[
            pl.BlockSpec((1, IDX_H * IDX_D, TQ), col),
            pl.BlockSpec((1, IDX_H, TQ), col),
            pl.BlockSpec((1, S, IDX_D), lambda b, i: (b, 0, 0)),
            pl.BlockSpec((1, D_A, TQ), col),
            pl.BlockSpec((1, KV_A, S, HD_A), lambda b, i: (b, 0, 0, 0)),
            pl.BlockSpec((1, n_kt, KV_A * HD_A, TK), lambda b, i: (b, 0, 0, 0)),
            pl.BlockSpec((1, 1, TQ), col),
            pl.BlockSpec((1, S, 1), lambda b, i: (b, 0, 0)),
        ],
        out_specs=pl.BlockSpec((1, D_A, TQ), col),
        out_shape=jax.ShapeDtypeStruct((B, D_A, S), f32),
        scratch_shapes=[
            pltpu.VMEM((S, TQ), f32),
            pltpu.VMEM((TK, TQ), f32),
            pltpu.VMEM((8, TQ), f32),
            pltpu.VMEM((H_A, TQ), f32),
            pltpu.VMEM((H_A, TQ), f32),
            pltpu.VMEM((H_A, HD_A, TQ), f32),
            pltpu.SMEM((1,), jnp.int32),
        ],
        compiler_params=params,
        name="attn_a",
    )(iqT, iwT, ik, qaT, ka, vaT, posr, posc)

    obT = pl.pallas_call(
        _attn_b_kernel,
        grid=grid_q,
        in_specs=[
            pl.BlockSpec((1, H_B * QB_PAD, TQ), col),
            pl.BlockSpec((1, H_B, S, QB_PAD), lambda b, i: (b, 0, 0, 0)),
            pl.BlockSpec((1, n_kt, D_B, TK), lambda b, i: (b, 0, 0, 0)),
        ],
        out_specs=pl.BlockSpec((1, D_B, TQ), col),
        out_shape=jax.ShapeDtypeStruct((B, D_B, S), f32),
        scratch_shapes=[
            pltpu.VMEM((H_B, TQ), f32),
            pltpu.VMEM((H_B, TQ), f32),
            pltpu.VMEM((H_B, V_B, TQ), f32),
        ],
        compiler_params=params,
        name="attn_b",
    )(qbT, kcat, vbT)

    return pl.pallas_call(
        functools.partial(_out_kernel, alpha=alpha),
        grid=grid_m,
        in_specs=[
            pl.BlockSpec((1, D_A, TM), col),
            pl.BlockSpec((1, D_B, TM), col),
            pl.BlockSpec((1, D_MIX, TM), col),
            pl.BlockSpec((1, TM, D_MODEL), tok),
            _full((D_MIX, D_MODEL)),
            _full((D_A, 1)),
            _full((D_B, 1)),
            _full((1, D_MODEL)),
            _full((1, D_MODEL)),
        ],
        out_specs=pl.BlockSpec((1, TM, D_MODEL), tok),
        out_shape=jax.ShapeDtypeStruct((B, S, D_MODEL), f32),
        compiler_params=params,
        name="out",
    )(oaT, obT, szT, x, w_out.astype(bf), g_out_a.reshape(D_A, 1), g_out_b.reshape(D_B, 1),
      ln_g.reshape(1, D_MODEL), ln_b.reshape(1, D_MODEL))


def kernel(x, positions, w_in, g_q_lat, w_uq, g_kv_lat, w_ukv, g_out_a, g_out_b, w_out, ln_g, ln_b):
    depth = w_in.shape[0]
    alpha = (2.0 * depth) ** 0.25
    posf = positions.astype(jnp.float32)
    posr, posc = posf[:, None, :], posf[:, :, None]
    h = x
    for l in range(depth):
        h = _layer(h, posr, posc, w_in[l], g_q_lat[l], w_uq[l], g_kv_lat[l], w_ukv[l],
                   g_out_a[l], g_out_b[l], w_out[l], ln_g[l], ln_b[l], alpha)
    return h
```

```python
import functools

import numpy as np
import jax
import jax.numpy as jnp
from jax import lax
from jax.experimental import pallas as pl
from jax.experimental.pallas import tpu as pltpu

D_MODEL = 1024
H_A, KV_A, HD_A = 8, 2, 64
REP_A = H_A // KV_A
D_A = H_A * HD_A
IDX_H, IDX_D = 8, 64
TOPK_MAX = 256
H_B, D_NOPE, D_ROPE, V_B = 8, 64, 32, 64
Q_LORA, KV_LORA = 384, 256
D_B = H_B * V_B
D_MIX = D_A + D_B
ROPE_THETA = 10000.0
NEG_INF = -1e30
IN_SIZES = (D_A, KV_A * HD_A, KV_A * HD_A, IDX_H * IDX_D, IDX_D, IDX_H, Q_LORA, KV_LORA, D_ROPE, D_MIX)
IW_SCALE = IDX_H ** -0.5 * IDX_D ** -0.5
LOG2E = 1.4426950408889634
SCALE_B = (D_NOPE + D_ROPE) ** -0.5 * LOG2E
QB_PAD = 128
SLOPES = tuple(2.0 ** (-8.0 * (h + 1) / H_A) for h in range(H_A))

TM = 512
TQ = 256
TK = 256
BISECT_FIXED = 22
BISECT_CAP = 320
VMEM_LIMIT = 48 * 1024 * 1024

_NT = (((1,), (1,)), ((), ()))
_TN = (((0,), (0,)), ((), ()))


def _rms_rows(v, g, eps=1e-6):
    return v * lax.rsqrt(jnp.mean(v * v, axis=-1, keepdims=True) + eps) * g


def _proj_kernel(x_ref, posr_ref, posc_ref, wqaT, wiqT, wzT, wvaT, wiwT, wka, wik, wcq, wckv, wkrp, wkrr,
                 gq, wuqT, gkv, wukp, wuvT, invr, invc,
                 qaT_o, ka_o, vaT_o, iqT_o, ik_o, iwT_o, qbT_o, kcat_o, vbT_o, szT_o):
    f32, bf = jnp.float32, jnp.bfloat16
    xb = x_ref[0].astype(bf)

    def nt(w):
        return lax.dot_general(w, xb, _NT, preferred_element_type=f32)

    def nn(w):
        return jnp.dot(xb, w, preferred_element_type=f32)

    qaT_o[0] = (nt(wqaT[...]) * LOG2E).astype(bf)
    iqT_o[0] = nt(wiqT[...]).astype(bf)
    iwT_o[0] = nt(wiwT[...]) * IW_SCALE
    va = nt(wvaT[...]).astype(bf)
    for c in range(TM // TK):
        vaT_o[0, c] = va[:, c * TK:(c + 1) * TK]
    z = nt(wzT[...])
    szT_o[0] = (z * jax.nn.sigmoid(z)).astype(bf)
    kk = nn(wka[...])
    for g in range(KV_A):
        ka_o[0, g] = kk[:, g * HD_A:(g + 1) * HD_A].astype(bf)
    ik_o[0] = nn(wik[...]).astype(bf)

    cqn = _rms_rows(nn(wcq[...]), gq[...]).astype(bf)
    qb = lax.dot_general(wuqT[...], cqn, _NT, preferred_element_type=f32)
    ang_t = invc[...] * posr_ref[0]
    cos_t, sin_t = jnp.cos(ang_t), jnp.sin(ang_t)
    half = D_ROPE // 2
    for h in range(H_B):
        b, o = h * (D_NOPE + D_ROPE), h * QB_PAD
        qbT_o[0, o:o + D_NOPE] = (qb[b:b + D_NOPE] * SCALE_B).astype(bf)
        x1 = qb[b + D_NOPE:b + D_NOPE + half]
        x2 = qb[b + D_NOPE + half:b + D_NOPE + D_ROPE]
        qbT_o[0, o + D_NOPE:o + D_NOPE + half] = ((x1 * cos_t - x2 * sin_t) * SCALE_B).astype(bf)
        qbT_o[0, o + D_NOPE + half:o + D_NOPE + D_ROPE] = ((x1 * sin_t + x2 * cos_t) * SCALE_B).astype(bf)
        qbT_o[0, o + D_NOPE + D_ROPE:o + QB_PAD] = jnp.zeros((QB_PAD - D_NOPE - D_ROPE, TM), bf)

    ckvn = _rms_rows(nn(wckv[...]), gkv[...]).astype(bf)
    knp = jnp.dot(ckvn, wukp[...], preferred_element_type=f32)
    ang = posc_ref[0] * invr[...]
    kro = nn(wkrp[...]) * jnp.cos(ang) + nn(wkrr[...]) * jnp.sin(ang)
    for h in range(H_B):
        kcat_o[0, h] = (knp[:, h * QB_PAD:(h + 1) * QB_PAD] + kro).astype(bf)
    vb = lax.dot_general(wuvT[...], ckvn, _NT, preferred_element_type=f32).astype(bf)
    for c in range(TM // TK):
        vbT_o[0, c] = vb[:, c * TK:(c + 1) * TK]


def _attn_a_kernel(iqT_ref, iwT_ref, ik_ref, qaT_ref, ka_ref, vaT_ref, posr_ref, posc_ref, o_ref,
                   isc_scr, bias_scr, st_scr, s_scr, p_scr, m_scr, l_scr, acc_scr, flag_smem, *, ktop):
    f32 = jnp.float32
    qi = pl.program_id(1)
    t_idx = qi * TQ + lax.broadcasted_iota(jnp.int32, (1, TQ), 1)
    iw = iwT_ref[0]

    def rows_of(j):
        return pl.ds(pl.multiple_of(j * TK, TK), TK)

    def idx_tile(j):
        k = ik_ref[0, rows_of(j), :]
        tot = jnp.zeros((TK, TQ), f32)
        for h in range(IDX_H):
            rel = jnp.dot(k, iqT_ref[0, h * IDX_D:(h + 1) * IDX_D, :], preferred_element_type=f32)
            tot = tot + jnp.maximum(rel, 0.0) * iw[h:h + 1, :]
        return tot

    def p1(j, amax):
        tot = idx_tile(j)
        isc_scr[rows_of(j), :] = tot
        return jnp.maximum(amax, jnp.max(jnp.abs(tot), axis=0, keepdims=True))

    amax = lax.fori_loop(0, qi, p1, jnp.zeros((1, TQ), f32))
    tot = idx_tile(qi)
    s_idx = qi * TK + lax.broadcasted_iota(jnp.int32, (TK, TQ), 0)
    causal = s_idx <= t_idx
    isc_scr[rows_of(qi), :] = jnp.where(causal, tot, NEG_INF)
    amax = jnp.maximum(amax, jnp.max(jnp.where(causal, jnp.abs(tot), 0.0), axis=0, keepdims=True))

    want = jnp.minimum(t_idx + 1, ktop).astype(f32)
    n_tiles = qi + 1

    def count_ge(mid):
        def body(j, acc8):
            ind = jnp.where(isc_scr[rows_of(j), :] >= mid, 1.0, 0.0)
            return acc8 + ind.reshape(TK // 8, 8, TQ).sum(axis=0)
        acc8 = lax.fori_loop(0, n_tiles, body, jnp.zeros((8, TQ), f32))
        return jnp.sum(acc8, axis=0, keepdims=True)

    def bisect(_, st):
        lo, hi, clo, chi = st
        mid = 0.5 * (lo + hi)
        c = count_ge(mid)
        ge = c >= want
        return (jnp.where(ge, mid, lo), jnp.where(ge, hi, mid), jnp.where(ge, c, clo), jnp.where(ge, chi, c))

    st0 = (-amax, amax * 1.0001 + 1e-30, (t_idx + 1).astype(f32), jnp.zeros((1, TQ), f32))
    lo, hi, clo, chi = lax.fori_loop(0, BISECT_FIXED, bisect, st0)
    st_scr[0:1, :] = lo
    st_scr[1:2, :] = hi
    st_scr[2:3, :] = clo
    st_scr[3:4, :] = chi

    def unresolved_any(clo, chi, distinct):
        u = jnp.where((clo - chi) > (want - chi), distinct, 0.0)
        return (jnp.max(u) > 0.5).astype(jnp.int32)

    def refine_cond(c):
        it, more = c
        return jnp.logical_and(more > 0, it < BISECT_CAP)

    def refine(c):
        it, _ = c
        lo, hi, clo, chi = st_scr[0:1, :], st_scr[1:2, :], st_scr[2:3, :], st_scr[3:4, :]
        mid = 0.5 * (lo + hi)

        def body(j, carry):
            a8, mn8, mx8 = carry
            x = isc_scr[rows_of(j), :]
            ind = jnp.where(x >= mid, 1.0, 0.0)
            inside = jnp.where(x >= lo, jnp.where(x < hi, 1.0, 0.0), 0.0) > 0.5
            mn = jnp.where(inside, x, 3e38).reshape(TK // 8, 8, TQ).min(axis=0)
            mx = jnp.where(inside, x, -3e38).reshape(TK // 8, 8, TQ).max(axis=0)
            return (a8 + ind.reshape(TK // 8, 8, TQ).sum(axis=0), jnp.minimum(mn8, mn), jnp.maximum(mx8, mx))

        a8, mn8, mx8 = lax.fori_loop(
            0, n_tiles, body,
            (jnp.zeros((8, TQ), f32), jnp.full((8, TQ), 3e38, f32), jnp.full((8, TQ), -3e38, f32)))
        cnt = jnp.sum(a8, axis=0, keepdims=True)
        mn = jnp.min(mn8, axis=0, keepdims=True)
        mx = jnp.max(mx8, axis=0, keepdims=True)
        ge = cnt >= want
        lo2, hi2 = jnp.where(ge, mid, lo), jnp.where(ge, hi, mid)
        clo2, chi2 = jnp.where(ge, cnt, clo), jnp.where(ge, chi, cnt)
        st_scr[0:1, :] = lo2
        st_scr[1:2, :] = hi2
        st_scr[2:3, :] = clo2
        st_scr[3:4, :] = chi2
        return it + 1, unresolved_any(clo2, chi2, jnp.where(mx > mn, 1.0, 0.0))

    lax.while_loop(refine_cond, refine,
                   (jnp.int32(BISECT_FIXED), unresolved_any(clo, chi, jnp.ones((1, TQ), f32))))
    lo, hi, clo, chi = st_scr[0:1, :], st_scr[1:2, :], st_scr[2:3, :], st_scr[3:4, :]
    need = want - chi
    flag_smem[0] = unresolved_any(clo, chi, jnp.ones((1, TQ), f32))
    st_scr[4:5, :] = need
    st_scr[5:6, :] = jnp.zeros((1, TQ), f32)

    m_scr[...] = jnp.full(m_scr.shape, -3e38, f32)
    l_scr[...] = jnp.zeros(l_scr.shape, f32)
    acc_scr[...] = jnp.zeros(acc_scr.shape, f32)
    pq = posr_ref[0]

    def p3(j, _):
        rows = rows_of(j)
        x = isc_scr[rows, :]
        nad = -jnp.abs(pq - posc_ref[0, rows, :])
        lo = st_scr[0:1, :]

        @pl.when(flag_smem[0] == 0)
        def _():
            bias_scr[...] = jnp.where(x >= lo, nad, NEG_INF)

        @pl.when(flag_smem[0] != 0)
        def _():
            hi, need, seen = st_scr[1:2, :], st_scr[4:5, :], st_scr[5:6, :]
            above = jnp.where(x >= hi, 1.0, 0.0)
            tied = jnp.where(x >= lo, 1.0, 0.0) - above
            r_i = lax.broadcasted_iota(jnp.int32, (TK, TK), 0)
            c_i = lax.broadcasted_iota(jnp.int32, (TK, TK), 1)
            tri = jnp.where(r_i >= c_i, 1.0, 0.0).astype(jnp.bfloat16)
            incl = jnp.dot(tri, tied.astype(jnp.bfloat16), preferred_element_type=f32)
            rank = incl - tied + seen
            keep = above + tied * jnp.where(rank < need, 1.0, 0.0)
            bias_scr[...] = jnp.where(keep > 0.5, nad, NEG_INF)
            st_scr[5:6, :] = seen + incl[TK - 1:TK, :]

        def score(h):
            qk = jnp.dot(ka_ref[0, h // REP_A, rows, :], qaT_ref[0, h * HD_A:(h + 1) * HD_A, :],
                         preferred_element_type=f32)
            return qk + (SLOPES[h] * LOG2E) * bias_scr[...]

        def value(h):
            g = h // REP_A
            return vaT_ref[0, j, g * HD_A:(g + 1) * HD_A, :]

        _softmax_pv_tile(H_A, score, value, s_scr, p_scr, m_scr, l_scr, acc_scr)
        return 0

    lax.fori_loop(0, n_tiles, p3, 0)
    for h in range(H_A):
        o_ref[0, h * HD_A:(h + 1) * HD_A, :] = acc_scr[h] / l_scr[h:h + 1, :]


def _softmax_pv_tile(nheads, score_fn, v_fn, s_scr, p_scr, m_scr, l_scr, acc_scr):
    f32 = jnp.float32
    tile_max = []
    for h in range(nheads):
        s = score_fn(h)
        s_scr[h] = s
        tile_max.append(jnp.max(s, axis=0, keepdims=True))
    alphas = []
    for h in range(nheads):
        m_old = m_scr[h:h + 1, :]
        m_new = jnp.maximum(m_old, tile_max[h])
        p = jnp.exp2(s_scr[h] - m_new)
        alpha = jnp.exp2(m_old - m_new)
        m_scr[h:h + 1, :] = m_new
        l_scr[h:h + 1, :] = alpha * l_scr[h:h + 1, :] + jnp.sum(p, axis=0, keepdims=True)
        p_scr[h] = p.astype(jnp.bfloat16)
        alphas.append(alpha)
    for h in range(nheads):
        acc_scr[h] = alphas[h] * acc_scr[h] + jnp.dot(v_fn(h), p_scr[h], preferred_element_type=f32)


def _attn_b_kernel(qbT_ref, kcat_ref, vbT_ref, o_ref, s_scr, p_scr, m_scr, l_scr, acc_scr):
    f32 = jnp.float32
    qi = pl.program_id(1)
    m_scr[...] = jnp.full(m_scr.shape, -3e38, f32)
    l_scr[...] = jnp.zeros(l_scr.shape, f32)
    acc_scr[...] = jnp.zeros(acc_scr.shape, f32)

    def tile(j, masked):
        rows = pl.ds(pl.multiple_of(j * TK, TK), TK)
        if masked:
            s_idx = j * TK + lax.broadcasted_iota(jnp.int32, (TK, TQ), 0)
            t_idx = qi * TQ + lax.broadcasted_iota(jnp.int32, (TK, TQ), 1)
            causal = s_idx <= t_idx

        def score(h):
            s = jnp.dot(kcat_ref[0, h, rows, :], qbT_ref[0, h * QB_PAD:(h + 1) * QB_PAD, :],
                        preferred_element_type=f32)
            return jnp.where(causal, s, NEG_INF) if masked else s

        _softmax_pv_tile(H_B, score, lambda h: vbT_ref[0, j, h * V_B:(h + 1) * V_B, :],
                         s_scr, p_scr, m_scr, l_scr, acc_scr)

    def body(j, _):
        tile(j, False)
        return 0

    lax.fori_loop(0, qi, body, 0)
    tile(qi, True)
    for h in range(H_B):
        o_ref[0, h * V_B:(h + 1) * V_B, :] = acc_scr[h] / l_scr[h:h + 1, :]


def _out_kernel(oaT_ref, obT_ref, szT_ref, x_ref, wout_ref, ga_ref, gb_ref, lng_ref, lnb_ref, o_ref, *, alpha):
    f32 = jnp.float32

    def rms_cols(v, g, eps=1e-6):
        return v * lax.rsqrt(jnp.mean(v * v, axis=0, keepdims=True) + eps) * g

    y_t = jnp.concatenate([rms_cols(oaT_ref[0], ga_ref[...]), rms_cols(obT_ref[0], gb_ref[...])], axis=0)
    y_t = (y_t * szT_ref[0].astype(f32)).astype(jnp.bfloat16)
    out = lax.dot_general(y_t, wout_ref[...], _TN, preferred_element_type=f32)
    hres = alpha * x_ref[0] + out
    mu = jnp.mean(hres, axis=-1, keepdims=True)
    d = hres - mu
    var = jnp.mean(d * d, axis=-1, keepdims=True)
    o_ref[0] = d * lax.rsqrt(var + 1e-5) * lng_ref[...] + lnb_ref[...]


def _full(shape):
    return pl.BlockSpec(shape, lambda *_: (0,) * len(shape))


def _layer(x, posr, posc, w_in, g_q_lat, w_uq, g_kv_lat, w_ukv, g_out_a, g_out_b, w_out, ln_g, ln_b, alpha):
    B, S, _ = x.shape
    f32, bf = jnp.float32, jnp.bfloat16
    assert S % TM == 0 and TM % TK == 0 and TQ == TK
    ktop = min(TOPK_MAX, S // 4)
    n_kt = S // TK

    offs = np.concatenate([[0], np.cumsum(IN_SIZES)])
    wqa, wka, wva, wiq, wik, wiw, wcq, wckv, wkr, wz = [w_in[:, offs[i]:offs[i + 1]] for i in range(10)]
    half = D_ROPE // 2
    pad = jnp.zeros((D_MODEL, QB_PAD), f32)
    wkrp = pad.at[:, D_NOPE:D_NOPE + D_ROPE].set(wkr)
    wkrr = pad.at[:, D_NOPE:D_NOPE + half].set(-wkr[:, half:]).at[:, D_NOPE + half:D_NOPE + D_ROPE].set(wkr[:, :half])
    wukv3 = w_ukv.reshape(KV_LORA, H_B, D_NOPE + V_B)
    wukp = jnp.zeros((KV_LORA, H_B, QB_PAD), f32).at[:, :, :D_NOPE].set(wukv3[:, :, :D_NOPE]).reshape(KV_LORA, H_B * QB_PAD)
    wuvT = wukv3[:, :, D_NOPE:].reshape(KV_LORA, D_B).T
    inv_freq = ROPE_THETA ** (-jnp.arange(0, D_ROPE, 2, dtype=f32) / D_ROPE)
    invr = jnp.zeros((1, QB_PAD), f32).at[0, D_NOPE:D_NOPE + half].set(inv_freq).at[0, D_NOPE + half:D_NOPE + D_ROPE].set(inv_freq)
    invc = inv_freq.reshape(half, 1)
    weights = [
        (wqa * HD_A ** -0.5).T.astype(bf), wiq.T.astype(bf), wz.T.astype(bf), wva.T.astype(bf), wiw.T.astype(bf),
        wka.astype(bf), wik.astype(bf), wcq.astype(bf), wckv.astype(bf), wkrp.astype(bf), wkrr.astype(bf),
        g_q_lat.reshape(1, Q_LORA), w_uq.T.astype(bf), g_kv_lat.reshape(1, KV_LORA), wukp.astype(bf), wuvT.astype(bf),
        invr, invc,
    ]

    grid_m = (B, S // TM)
    tok = lambda b, i: (b, i, 0)
    col = lambda b, i: (b, 0, i)
    proj_out_shapes = [
        jax.ShapeDtypeStruct((B, D_A, S), bf),
        jax.ShapeDtypeStruct((B, KV_A, S, HD_A), bf),
        jax.ShapeDtypeStruct((B, n_kt, KV_A * HD_A, TK), bf),
        jax.ShapeDtypeStruct((B, IDX_H * IDX_D, S), bf),
        jax.ShapeDtypeStruct((B, S, IDX_D), bf),
        jax.ShapeDtypeStruct((B, IDX_H, S), f32),
        jax.ShapeDtypeStruct((B, H_B * QB_PAD, S), bf),
        jax.ShapeDtypeStruct((B, H_B, S, QB_PAD), bf),
        jax.ShapeDtypeStruct((B, n_kt, D_B, TK), bf),
        jax.ShapeDtypeStruct((B, D_MIX, S), bf),
    ]
    proj_out_specs = [
        pl.BlockSpec((1, D_A, TM), col),
        pl.BlockSpec((1, KV_A, TM, HD_A), lambda b, i: (b, 0, i, 0)),
        pl.BlockSpec((1, TM // TK, KV_A * HD_A, TK), lambda b, i: (b, i, 0, 0)),
        pl.BlockSpec((1, IDX_H * IDX_D, TM), col),
        pl.BlockSpec((1, TM, IDX_D), tok),
        pl.BlockSpec((1, IDX_H, TM), col),
        pl.BlockSpec((1, H_B * QB_PAD, TM), col),
        pl.BlockSpec((1, H_B, TM, QB_PAD), lambda b, i: (b, 0, i, 0)),
        pl.BlockSpec((1, TM // TK, D_B, TK), lambda b, i: (b, i, 0, 0)),
        pl.BlockSpec((1, D_MIX, TM), col),
    ]
    params = pltpu.CompilerParams(dimension_semantics=("parallel", "arbitrary"), vmem_limit_bytes=VMEM_LIMIT)
    qaT, ka, vaT, iqT, ik, iwT, qbT, kcat, vbT, szT = pl.pallas_call(
        _proj_kernel,
        grid=grid_m,
        in_specs=[pl.BlockSpec((1, TM, D_MODEL), tok), pl.BlockSpec((1, 1, TM), col), pl.BlockSpec((1, TM, 1), tok)]
        + [_full(w.shape) for w in weights],
        out_specs=proj_out_specs,
        out_shape=proj_out_shapes,
        compiler_params=params,
        name="proj",
    )(x, posr, posc, *weights)

    grid_q = (B, S // TQ)
    oaT = pl.pallas_call(
        functools.partial(_attn_a_kernel, ktop=ktop),
        grid=grid_q,
        in_specs=[
            pl.BlockSpec((1, IDX_H * IDX_D, TQ), col),
            pl.BlockSpec((1, IDX_H, TQ), col),
            pl.BlockSpec((1, S, IDX_D), lambda b, i: (b, 0, 0)),
            pl.BlockSpec((1, D_A, TQ), col),
            pl.BlockSpec((1, KV_A, S, HD_A), lambda b, i: (b, 0, 0, 0)),
            pl.BlockSpec((1, n_kt, KV_A * HD_A, TK), lambda b, i: (b, 0, 0, 0)),
            pl.BlockSpec((1, 1, TQ), col),
            pl.BlockSpec((1, S, 1), lambda b, i: (b, 0, 0)),
        ],
        out_specs=pl.BlockSpec((1, D_A, TQ), col),
        out_shape=jax.ShapeDtypeStruct((B, D_A, S), f32),
        scratch_shapes=[
            pltpu.VMEM((S, TQ), f32),
            pltpu.VMEM((TK, TQ), f32),
            pltpu.VMEM((8, TQ), f32),
            pltpu.VMEM((H_A, TK, TQ), f32),
            pltpu.VMEM((H_A, TK, TQ), bf),
            pltpu.VMEM((H_A, TQ), f32),
            pltpu.VMEM((H_A, TQ), f32),
            pltpu.VMEM((H_A, HD_A, TQ), f32),
            pltpu.SMEM((1,), jnp.int32),
        ],
        compiler_params=params,
        name="attn_a",
    )(iqT, iwT, ik, qaT, ka, vaT, posr, posc)

    obT = pl.pallas_call(
        _attn_b_kernel,
        grid=grid_q,
        in_specs=[
            pl.BlockSpec((1, H_B * QB_PAD, TQ), col),
            pl.BlockSpec((1, H_B, S, QB_PAD), lambda b, i: (b, 0, 0, 0)),
            pl.BlockSpec((1, n_kt, D_B, TK), lambda b, i: (b, 0, 0, 0)),
        ],
        out_specs=pl.BlockSpec((1, D_B, TQ), col),
        out_shape=jax.ShapeDtypeStruct((B, D_B, S), f32),
        scratch_shapes=[
            pltpu.VMEM((H_B, TK, TQ), f32),
            pltpu.VMEM((H_B, TK, TQ), bf),
            pltpu.VMEM((H_B, TQ), f32),
            pltpu.VMEM((H_B, TQ), f32),
            pltpu.VMEM((H_B, V_B, TQ), f32),
        ],
        compiler_params=params,
        name="attn_b",
    )(qbT, kcat, vbT)

    return pl.pallas_call(
        functools.partial(_out_kernel, alpha=alpha),
        grid=grid_m,
        in_specs=[
            pl.BlockSpec((1, D_A, TM), col),
            pl.BlockSpec((1, D_B, TM), col),
            pl.BlockSpec((1, D_MIX, TM), col),
            pl.BlockSpec((1, TM, D_MODEL), tok),
            _full((D_MIX, D_MODEL)),
            _full((D_A, 1)),
            _full((D_B, 1)),
            _full((1, D_MODEL)),
            _full((1, D_MODEL)),
        ],
        out_specs=pl.BlockSpec((1, TM, D_MODEL), tok),
        out_shape=jax.ShapeDtypeStruct((B, S, D_MODEL), f32),
        compiler_params=params,
        name="out",
    )(oaT, obT, szT, x, w_out.astype(bf), g_out_a.reshape(D_A, 1), g_out_b.reshape(D_B, 1),
      ln_g.reshape(1, D_MODEL), ln_b.reshape(1, D_MODEL))


def kernel(x, positions, w_in, g_q_lat, w_uq, g_kv_lat, w_ukv, g_out_a, g_out_b, w_out, ln_g, ln_b):
    depth = w_in.shape[0]
    alpha = (2.0 * depth) ** 0.25
    posf = positions.astype(jnp.float32)
    posr, posc = posf[:, None, :], posf[:, :, None]
    h = x
    for l in range(depth):
        h = _layer(h, posr, posc, w_in[l], g_q_lat[l], w_uq[l], g_kv_lat[l], w_ukv[l],
                   g_out_a[l], g_out_b[l], w_out[l], ln_g[l], ln_b[l], alpha)
    return h
```

```python
import functools

import numpy as np
import jax
import jax.numpy as jnp
from jax import lax
from jax.experimental import pallas as pl
from jax.experimental.pallas import tpu as pltpu

D_MODEL = 1024
H_A, KV_A, HD_A = 8, 2, 64
REP_A = H_A // KV_A
D_A = H_A * HD_A
IDX_H, IDX_D = 8, 64
TOPK_MAX = 256
H_B, D_NOPE, D_ROPE, V_B = 8, 64, 32, 64
Q_LORA, KV_LORA = 384, 256
D_B = H_B * V_B
D_MIX = D_A + D_B
ROPE_THETA = 10000.0
NEG_INF = -1e30
IN_SIZES = (D_A, KV_A * HD_A, KV_A * HD_A, IDX_H * IDX_D, IDX_D, IDX_H, Q_LORA, KV_LORA, D_ROPE, D_MIX)
IW_SCALE = IDX_H ** -0.5 * IDX_D ** -0.5
LOG2E = 1.4426950408889634
SCALE_B = (D_NOPE + D_ROPE) ** -0.5 * LOG2E
QB_PAD = 128
SLOPES = tuple(2.0 ** (-8.0 * (h + 1) / H_A) for h in range(H_A))

TM = 512
TQ = 256
TK = 256
CHAINS = 4
BISECT_FIXED = 22
BISECT_CAP = 320
VMEM_LIMIT = 48 * 1024 * 1024

_NT = (((1,), (1,)), ((), ()))
_TN = (((0,), (0,)), ((), ()))


def _rms_rows(v, g, eps=1e-6):
    return v * lax.rsqrt(jnp.mean(v * v, axis=-1, keepdims=True) + eps) * g


def _proj_kernel(x_ref, posr_ref, posc_ref, wqaT, wiqT, wzT, wvaT, wiwT, wka, wik, wcq, wckv, wkrp, wkrr,
                 gq, wuqT, gkv, wukp, wuvT, invr, invc,
                 qaT_o, ka_o, vaT_o, iqT_o, ik_o, iwT_o, qbT_o, kcat_o, vbT_o, szT_o):
    f32, bf = jnp.float32, jnp.bfloat16
    xb = x_ref[0].astype(bf)

    def nt(w):
        return lax.dot_general(w, xb, _NT, preferred_element_type=f32)

    def nn(w):
        return jnp.dot(xb, w, preferred_element_type=f32)

    qaT_o[0] = (nt(wqaT[...]) * LOG2E).astype(bf)
    iqT_o[0] = nt(wiqT[...]).astype(bf)
    iwT_o[0] = nt(wiwT[...]) * IW_SCALE
    va = nt(wvaT[...]).astype(bf)
    for c in range(TM // TK):
        vaT_o[0, c] = va[:, c * TK:(c + 1) * TK]
    z = nt(wzT[...])
    szT_o[0] = (z * jax.nn.sigmoid(z)).astype(bf)
    kk = nn(wka[...])
    for g in range(KV_A):
        ka_o[0, g] = kk[:, g * HD_A:(g + 1) * HD_A].astype(bf)
    ik_o[0] = nn(wik[...]).astype(bf)

    cqn = _rms_rows(nn(wcq[...]), gq[...]).astype(bf)
    qb = lax.dot_general(wuqT[...], cqn, _NT, preferred_element_type=f32)
    ang_t = invc[...] * posr_ref[0]
    cos_t, sin_t = jnp.cos(ang_t), jnp.sin(ang_t)
    half = D_ROPE // 2
    for h in range(H_B):
        b, o = h * (D_NOPE + D_ROPE), h * QB_PAD
        qbT_o[0, o:o + D_NOPE] = (qb[b:b + D_NOPE] * SCALE_B).astype(bf)
        x1 = qb[b + D_NOPE:b + D_NOPE + half]
        x2 = qb[b + D_NOPE + half:b + D_NOPE + D_ROPE]
        qbT_o[0, o + D_NOPE:o + D_NOPE + half] = ((x1 * cos_t - x2 * sin_t) * SCALE_B).astype(bf)
        qbT_o[0, o + D_NOPE + half:o + D_NOPE + D_ROPE] = ((x1 * sin_t + x2 * cos_t) * SCALE_B).astype(bf)
        qbT_o[0, o + D_NOPE + D_ROPE:o + QB_PAD] = jnp.zeros((QB_PAD - D_NOPE - D_ROPE, TM), bf)

    ckvn = _rms_rows(nn(wckv[...]), gkv[...]).astype(bf)
    knp = jnp.dot(ckvn, wukp[...], preferred_element_type=f32)
    ang = posc_ref[0] * invr[...]
    kro = nn(wkrp[...]) * jnp.cos(ang) + nn(wkrr[...]) * jnp.sin(ang)
    for h in range(H_B):
        kcat_o[0, h] = (knp[:, h * QB_PAD:(h + 1) * QB_PAD] + kro).astype(bf)
    vb = lax.dot_general(wuvT[...], ckvn, _NT, preferred_element_type=f32).astype(bf)
    for c in range(TM // TK):
        vbT_o[0, c] = vb[:, c * TK:(c + 1) * TK]


def _qk_stage(nheads, score_fn, s_scr, tmax_scr):
    for h in range(nheads):
        s = score_fn(h)
        s_scr[h] = s
        tmax_scr[h:h + 1, :] = jnp.max(s, axis=0, keepdims=True)


def _softmax_stage(nheads, s_scr, tmax_scr, p_scr, m_scr, l_scr):
    alphas = []
    for h in range(nheads):
        m_old = m_scr[h:h + 1, :]
        m_new = jnp.maximum(m_old, tmax_scr[h:h + 1, :])
        p = jnp.exp2(s_scr[h] - m_new)
        alpha = jnp.exp2(m_old - m_new)
        m_scr[h:h + 1, :] = m_new
        l_scr[h:h + 1, :] = alpha * l_scr[h:h + 1, :] + jnp.sum(p, axis=0, keepdims=True)
        p_scr[h] = p.astype(jnp.bfloat16)
        alphas.append(alpha)
    return alphas


def _pv_stage(nheads, v_fn, alphas, p_scr, acc_scr):
    for h in range(nheads):
        acc_scr[h] = alphas[h] * acc_scr[h] + jnp.dot(v_fn(h), p_scr[h], preferred_element_type=jnp.float32)


def _attn_a_kernel(iqT_ref, iwT_ref, ik_ref, qaT_ref, ka_ref, vaT_ref, posr_ref, posc_ref, o_ref,
                   isc_scr, bias_scr, st_scr, s_scr, tmax_scr, p_scr, m_scr, l_scr, acc_scr, flag_smem, *, ktop):
    f32 = jnp.float32
    qi = pl.program_id(1)
    t_idx = qi * TQ + lax.broadcasted_iota(jnp.int32, (1, TQ), 1)
    iw = iwT_ref[0]

    def rows_of(j):
        return pl.ds(pl.multiple_of(j * TK, TK), TK)

    def idx_tile(j):
        k = ik_ref[0, rows_of(j), :]
        tot = jnp.zeros((TK, TQ), f32)
        for h in range(IDX_H):
            rel = jnp.dot(k, iqT_ref[0, h * IDX_D:(h + 1) * IDX_D, :], preferred_element_type=f32)
            tot = tot + jnp.maximum(rel, 0.0) * iw[h:h + 1, :]
        return tot

    def p1(j, amax):
        tot = idx_tile(j)
        isc_scr[rows_of(j), :] = tot
        return jnp.maximum(amax, jnp.max(jnp.abs(tot), axis=0, keepdims=True))

    amax = lax.fori_loop(0, qi, p1, jnp.zeros((1, TQ), f32))
    tot = idx_tile(qi)
    s_idx = qi * TK + lax.broadcasted_iota(jnp.int32, (TK, TQ), 0)
    causal = s_idx <= t_idx
    isc_scr[rows_of(qi), :] = jnp.where(causal, tot, NEG_INF)
    amax = jnp.maximum(amax, jnp.max(jnp.where(causal, jnp.abs(tot), 0.0), axis=0, keepdims=True))

    want = jnp.minimum(t_idx + 1, ktop).astype(f32)
    n_tiles = qi + 1

    def count_ge(mid):
        def body(j, acc):
            ind = jnp.where(isc_scr[rows_of(j), :] >= mid, 1.0, 0.0)
            return acc + ind.reshape(CHAINS, TK // (8 * CHAINS), 8, TQ).sum(axis=1)
        acc = lax.fori_loop(0, n_tiles, body, jnp.zeros((CHAINS, 8, TQ), f32))
        return jnp.sum(acc.reshape(CHAINS * 8, TQ), axis=0, keepdims=True)

    def bisect(_, st):
        lo, hi, clo, chi = st
        mid = 0.5 * (lo + hi)
        c = count_ge(mid)
        ge = c >= want
        return (jnp.where(ge, mid, lo), jnp.where(ge, hi, mid), jnp.where(ge, c, clo), jnp.where(ge, chi, c))

    st0 = (-amax, amax * 1.0001 + 1e-30, (t_idx + 1).astype(f32), jnp.zeros((1, TQ), f32))
    lo, hi, clo, chi = lax.fori_loop(0, BISECT_FIXED, bisect, st0)
    st_scr[0:1, :] = lo
    st_scr[1:2, :] = hi
    st_scr[2:3, :] = clo
    st_scr[3:4, :] = chi

    def unresolved_any(clo, chi, distinct):
        u = jnp.where((clo - chi) > (want - chi), distinct, 0.0)
        return (jnp.max(u) > 0.5).astype(jnp.int32)

    def refine_cond(c):
        it, more = c
        return jnp.logical_and(more > 0, it < BISECT_CAP)

    def refine(c):
        it, _ = c
        lo, hi, clo, chi = st_scr[0:1, :], st_scr[1:2, :], st_scr[2:3, :], st_scr[3:4, :]
        mid = 0.5 * (lo + hi)

        def body(j, carry):
            a8, mn8, mx8 = carry
            x = isc_scr[rows_of(j), :]
            ind = jnp.where(x >= mid, 1.0, 0.0)
            inside = jnp.where(x >= lo, jnp.where(x < hi, 1.0, 0.0), 0.0) > 0.5
            mn = jnp.where(inside, x, 3e38).reshape(TK // 8, 8, TQ).min(axis=0)
            mx = jnp.where(inside, x, -3e38).reshape(TK // 8, 8, TQ).max(axis=0)
            return (a8 + ind.reshape(TK // 8, 8, TQ).sum(axis=0), jnp.minimum(mn8, mn), jnp.maximum(mx8, mx))

        a8, mn8, mx8 = lax.fori_loop(
            0, n_tiles, body,
            (jnp.zeros((8, TQ), f32), jnp.full((8, TQ), 3e38, f32), jnp.full((8, TQ), -3e38, f32)))
        cnt = jnp.sum(a8, axis=0, keepdims=True)
        mn = jnp.min(mn8, axis=0, keepdims=True)
        mx = jnp.max(mx8, axis=0, keepdims=True)
        ge = cnt >= want
        lo2, hi2 = jnp.where(ge, mid, lo), jnp.where(ge, hi, mid)
        clo2, chi2 = jnp.where(ge, cnt, clo), jnp.where(ge, chi, cnt)
        st_scr[0:1, :] = lo2
        st_scr[1:2, :] = hi2
        st_scr[2:3, :] = clo2
        st_scr[3:4, :] = chi2
        return it + 1, unresolved_any(clo2, chi2, jnp.where(mx > mn, 1.0, 0.0))

    lax.while_loop(refine_cond, refine,
                   (jnp.int32(BISECT_FIXED), unresolved_any(clo, chi, jnp.ones((1, TQ), f32))))
    lo, hi, clo, chi = st_scr[0:1, :], st_scr[1:2, :], st_scr[2:3, :], st_scr[3:4, :]
    need = want - chi
    flag_smem[0] = unresolved_any(clo, chi, jnp.ones((1, TQ), f32))
    st_scr[4:5, :] = need
    st_scr[5:6, :] = jnp.zeros((1, TQ), f32)

    m_scr[...] = jnp.full(m_scr.shape, -3e38, f32)
    l_scr[...] = jnp.zeros(l_scr.shape, f32)
    acc_scr[...] = jnp.zeros(acc_scr.shape, f32)
    pq = posr_ref[0]

    def select_bias(j):
        rows = rows_of(j)
        x = isc_scr[rows, :]
        nad = -jnp.abs(pq - posc_ref[0, rows, :])
        lo = st_scr[0:1, :]

        @pl.when(flag_smem[0] == 0)
        def _():
            bias_scr[...] = jnp.where(x >= lo, nad, NEG_INF)

        @pl.when(flag_smem[0] != 0)
        def _():
            hi, need, seen = st_scr[1:2, :], st_scr[4:5, :], st_scr[5:6, :]
            above = jnp.where(x >= hi, 1.0, 0.0)
            tied = jnp.where(x >= lo, 1.0, 0.0) - above
            r_i = lax.broadcasted_iota(jnp.int32, (TK, TK), 0)
            c_i = lax.broadcasted_iota(jnp.int32, (TK, TK), 1)
            tri = jnp.where(r_i >= c_i, 1.0, 0.0).astype(jnp.bfloat16)
            incl = jnp.dot(tri, tied.astype(jnp.bfloat16), preferred_element_type=f32)
            rank = incl - tied + seen
            keep = above + tied * jnp.where(rank < need, 1.0, 0.0)
            bias_scr[...] = jnp.where(keep > 0.5, nad, NEG_INF)
            st_scr[5:6, :] = seen + incl[TK - 1:TK, :]

    def qk(j):
        rows = rows_of(j)

        def score(h):
            qk_h = jnp.dot(ka_ref[0, h // REP_A, rows, :], qaT_ref[0, h * HD_A:(h + 1) * HD_A, :],
                           preferred_element_type=f32)
            return qk_h + (SLOPES[h] * LOG2E) * bias_scr[...]

        _qk_stage(H_A, score, s_scr, tmax_scr)

    def softmax():
        return _softmax_stage(H_A, s_scr, tmax_scr, p_scr, m_scr, l_scr)

    def pv(j, alphas):
        def value(h):
            g = h // REP_A
            return vaT_ref[0, j, g * HD_A:(g + 1) * HD_A, :]

        _pv_stage(H_A, value, alphas, p_scr, acc_scr)

    select_bias(0)
    qk(0)

    def p3(j, _):
        select_bias(j + 1)
        alphas = softmax()
        qk(j + 1)
        pv(j, alphas)
        return 0

    lax.fori_loop(0, qi, p3, 0)
    pv(qi, softmax())
    for h in range(H_A):
        o_ref[0, h * HD_A:(h + 1) * HD_A, :] = acc_scr[h] / l_scr[h:h + 1, :]


def _attn_b_kernel(qbT_ref, kcat_ref, vbT_ref, o_ref, s_scr, tmax_scr, p_scr, m_scr, l_scr, acc_scr):
    f32 = jnp.float32
    qi = pl.program_id(1)
    m_scr[...] = jnp.full(m_scr.shape, -3e38, f32)
    l_scr[...] = jnp.zeros(l_scr.shape, f32)
    acc_scr[...] = jnp.zeros(acc_scr.shape, f32)

    def qk(j, masked):
        rows = pl.ds(pl.multiple_of(j * TK, TK), TK)
        if masked:
            s_idx = j * TK + lax.broadcasted_iota(jnp.int32, (TK, TQ), 0)
            t_idx = qi * TQ + lax.broadcasted_iota(jnp.int32, (TK, TQ), 1)
            causal = s_idx <= t_idx

        def score(h):
            s = jnp.dot(kcat_ref[0, h, rows, :], qbT_ref[0, h * QB_PAD:(h + 1) * QB_PAD, :],
                        preferred_element_type=f32)
            return jnp.where(causal, s, NEG_INF) if masked else s

        _qk_stage(H_B, score, s_scr, tmax_scr)

    def softmax():
        return _softmax_stage(H_B, s_scr, tmax_scr, p_scr, m_scr, l_scr)

    def pv(j, alphas):
        _pv_stage(H_B, lambda h: vbT_ref[0, j, h * V_B:(h + 1) * V_B, :], alphas, p_scr, acc_scr)

    qk(qi, True)

    def body(i, _):
        alphas = softmax()
        qk(i, False)
        pv(jnp.where(i == 0, qi, i - 1), alphas)
        return 0

    lax.fori_loop(0, qi, body, 0)
    pv(jnp.maximum(qi - 1, 0), softmax())
    for h in range(H_B):
        o_ref[0, h * V_B:(h + 1) * V_B, :] = acc_scr[h] / l_scr[h:h + 1, :]


def _out_kernel(oaT_ref, obT_ref, szT_ref, x_ref, wout_ref, ga_ref, gb_ref, lng_ref, lnb_ref, o_ref, *, alpha):
    f32 = jnp.float32

    def rms_cols(v, g, eps=1e-6):
        return v * lax.rsqrt(jnp.mean(v * v, axis=0, keepdims=True) + eps) * g

    y_t = jnp.concatenate([rms_cols(oaT_ref[0], ga_ref[...]), rms_cols(obT_ref[0], gb_ref[...])], axis=0)
    y_t = (y_t * szT_ref[0].astype(f32)).astype(jnp.bfloat16)
    out = lax.dot_general(y_t, wout_ref[...], _TN, preferred_element_type=f32)
    hres = alpha * x_ref[0] + out
    mu = jnp.mean(hres, axis=-1, keepdims=True)
    d = hres - mu
    var = jnp.mean(d * d, axis=-1, keepdims=True)
    o_ref[0] = d * lax.rsqrt(var + 1e-5) * lng_ref[...] + lnb_ref[...]


def _full(shape):
    return pl.BlockSpec(shape, lambda *_: (0,) * len(shape))


def _layer(x, posr, posc, w_in, g_q_lat, w_uq, g_kv_lat, w_ukv, g_out_a, g_out_b, w_out, ln_g, ln_b, alpha):
    B, S, _ = x.shape
    f32, bf = jnp.float32, jnp.bfloat16
    assert S % TM == 0 and TM % TK == 0 and TQ == TK
    ktop = min(TOPK_MAX, S // 4)
    n_kt = S // TK

    offs = np.concatenate([[0], np.cumsum(IN_SIZES)])
    wqa, wka, wva, wiq, wik, wiw, wcq, wckv, wkr, wz = [w_in[:, offs[i]:offs[i + 1]] for i in range(10)]
    half = D_ROPE // 2
    pad = jnp.zeros((D_MODEL, QB_PAD), f32)
    wkrp = pad.at[:, D_NOPE:D_NOPE + D_ROPE].set(wkr)
    wkrr = pad.at[:, D_NOPE:D_NOPE + half].set(-wkr[:, half:]).at[:, D_NOPE + half:D_NOPE + D_ROPE].set(wkr[:, :half])
    wukv3 = w_ukv.reshape(KV_LORA, H_B, D_NOPE + V_B)
    wukp = jnp.zeros((KV_LORA, H_B, QB_PAD), f32).at[:, :, :D_NOPE].set(wukv3[:, :, :D_NOPE]).reshape(KV_LORA, H_B * QB_PAD)
    wuvT = wukv3[:, :, D_NOPE:].reshape(KV_LORA, D_B).T
    inv_freq = ROPE_THETA ** (-jnp.arange(0, D_ROPE, 2, dtype=f32) / D_ROPE)
    invr = jnp.zeros((1, QB_PAD), f32).at[0, D_NOPE:D_NOPE + half].set(inv_freq).at[0, D_NOPE + half:D_NOPE + D_ROPE].set(inv_freq)
    invc = inv_freq.reshape(half, 1)
    weights = [
        (wqa * HD_A ** -0.5).T.astype(bf), wiq.T.astype(bf), wz.T.astype(bf), wva.T.astype(bf), wiw.T.astype(bf),
        wka.astype(bf), wik.astype(bf), wcq.astype(bf), wckv.astype(bf), wkrp.astype(bf), wkrr.astype(bf),
        g_q_lat.reshape(1, Q_LORA), w_uq.T.astype(bf), g_kv_lat.reshape(1, KV_LORA), wukp.astype(bf), wuvT.astype(bf),
        invr, invc,
    ]

    grid_m = (B, S // TM)
    tok = lambda b, i: (b, i, 0)
    col = lambda b, i: (b, 0, i)
    proj_out_shapes = [
        jax.ShapeDtypeStruct((B, D_A, S), bf),
        jax.ShapeDtypeStruct((B, KV_A, S, HD_A), bf),
        jax.ShapeDtypeStruct((B, n_kt, KV_A * HD_A, TK), bf),
        jax.ShapeDtypeStruct((B, IDX_H * IDX_D, S), bf),
        jax.ShapeDtypeStruct((B, S, IDX_D), bf),
        jax.ShapeDtypeStruct((B, IDX_H, S), f32),
        jax.ShapeDtypeStruct((B, H_B * QB_PAD, S), bf),
        jax.ShapeDtypeStruct((B, H_B, S, QB_PAD), bf),
        jax.ShapeDtypeStruct((B, n_kt, D_B, TK), bf),
        jax.ShapeDtypeStruct((B, D_MIX, S), bf),
    ]
    proj_out_specs = [
        pl.BlockSpec((1, D_A, TM), col),
        pl.BlockSpec((1, KV_A, TM, HD_A), lambda b, i: (b, 0, i, 0)),
        pl.BlockSpec((1, TM // TK, KV_A * HD_A, TK), lambda b, i: (b, i, 0, 0)),
        pl.BlockSpec((1, IDX_H * IDX_D, TM), col),
        pl.BlockSpec((1, TM, IDX_D), tok),
        pl.BlockSpec((1, IDX_H, TM), col),
        pl.BlockSpec((1, H_B * QB_PAD, TM), col),
        pl.BlockSpec((1, H_B, TM, QB_PAD), lambda b, i: (b, 0, i, 0)),
        pl.BlockSpec((1, TM // TK, D_B, TK), lambda b, i: (b, i, 0, 0)),
        pl.BlockSpec((1, D_MIX, TM), col),
    ]
    params = pltpu.CompilerParams(dimension_semantics=("parallel", "arbitrary"), vmem_limit_bytes=VMEM_LIMIT)
    qaT, ka, vaT, iqT, ik, iwT, qbT, kcat, vbT, szT = pl.pallas_call(
        _proj_kernel,
        grid=grid_m,
        in_specs=[pl.BlockSpec((1, TM, D_MODEL), tok), pl.BlockSpec((1, 1, TM), col), pl.BlockSpec((1, TM, 1), tok)]
        + [_full(w.shape) for w in weights],
        out_specs=proj_out_specs,
        out_shape=proj_out_shapes,
        compiler_params=params,
        name="proj",
    )(x, posr, posc, *weights)

    grid_q = (B, S // TQ)
    oaT = pl.pallas_call(
        functools.partial(_attn_a_kernel, ktop=ktop),
        grid=grid_q,
        in_specs=[
            pl.BlockSpec((1, IDX_H * IDX_D, TQ), col),
            pl.BlockSpec((1, IDX_H, TQ), col),
            pl.BlockSpec((1, S, IDX_D), lambda b, i: (b, 0, 0)),
            pl.BlockSpec((1, D_A, TQ), col),
            pl.BlockSpec((1, KV_A, S, HD_A), lambda b, i: (b, 0, 0, 0)),
            pl.BlockSpec((1, n_kt, KV_A * HD_A, TK), lambda b, i: (b, 0, 0, 0)),
            pl.BlockSpec((1, 1, TQ), col),
            pl.BlockSpec((1, S, 1), lambda b, i: (b, 0, 0)),
        ],
        out_specs=pl.BlockSpec((1, D_A, TQ), col),
        out_shape=jax.ShapeDtypeStruct((B, D_A, S), f32),
        scratch_shapes=[
            pltpu.VMEM((S, TQ), f32),
            pltpu.VMEM((TK, TQ), f32),
            pltpu.VMEM((8, TQ), f32),
            pltpu.VMEM((H_A, TK, TQ), f32),
            pltpu.VMEM((H_A, TQ), f32),
            pltpu.VMEM((H_A, TK, TQ), bf),
            pltpu.VMEM((H_A, TQ), f32),
            pltpu.VMEM((H_A, TQ), f32),
            pltpu.VMEM((H_A, HD_A, TQ), f32),
            pltpu.SMEM((1,), jnp.int32),
        ],
        compiler_params=params,
        name="attn_a",
    )(iqT, iwT, ik, qaT, ka, vaT, posr, posc)

    obT = pl.pallas_call(
        _attn_b_kernel,
        grid=grid_q,
        in_specs=[
            pl.BlockSpec((1, H_B * QB_PAD, TQ), col),
            pl.BlockSpec((1, H_B, S, QB_PAD), lambda b, i: (b, 0, 0, 0)),
            pl.BlockSpec((1, n_kt, D_B, TK), lambda b, i: (b, 0, 0, 0)),
        ],
        out_specs=pl.BlockSpec((1, D_B, TQ), col),
        out_shape=jax.ShapeDtypeStruct((B, D_B, S), f32),
        scratch_shapes=[
            pltpu.VMEM((H_B, TK, TQ), f32),
            pltpu.VMEM((H_B, TQ), f32),
            pltpu.VMEM((H_B, TK, TQ), bf),
            pltpu.VMEM((H_B, TQ), f32),
            pltpu.VMEM((H_B, TQ), f32),
            pltpu.VMEM((H_B, V_B, TQ), f32),
        ],
        compiler_params=params,
        name="attn_b",
    )(qbT, kcat, vbT)

    return pl.pallas_call(
        functools.partial(_out_kernel, alpha=alpha),
        grid=grid_m,
        in_specs=[
            pl.BlockSpec((1, D_A, TM), col),
            pl.BlockSpec((1, D_B, TM), col),
            pl.BlockSpec((1, D_MIX, TM), col),
            pl.BlockSpec((1, TM, D_MODEL), tok),
            _full((D_MIX, D_MODEL)),
            _full((D_A, 1)),
            _full((D_B, 1)),
            _full((1, D_MODEL)),
            _full((1, D_MODEL)),
        ],
        out_specs=pl.BlockSpec((1, TM, D_MODEL), tok),
        out_shape=jax.ShapeDtypeStruct((B, S, D_MODEL), f32),
        compiler_params=params,
        name="out",
    )(oaT, obT, szT, x, w_out.astype(bf), g_out_a.reshape(D_A, 1), g_out_b.reshape(D_B, 1),
      ln_g.reshape(1, D_MODEL), ln_b.reshape(1, D_MODEL))


def kernel(x, positions, w_in, g_q_lat, w_uq, g_kv_lat, w_ukv, g_out_a, g_out_b, w_out, ln_g, ln_b):
    depth = w_in.shape[0]
    alpha = (2.0 * depth) ** 0.25
    posf = positions.astype(jnp.float32)
    posr, posc = posf[:, None, :], posf[:, :, None]
    h = x
    for l in range(depth):
        h = _layer(h, posr, posc, w_in[l], g_q_lat[l], w_uq[l], g_kv_lat[l], w_ukv[l],
                   g_out_a[l], g_out_b[l], w_out[l], ln_g[l], ln_b[l], alpha)
    return h
```

```python
import functools

import numpy as np
import jax
import jax.numpy as jnp
from jax import lax
from jax.experimental import pallas as pl
from jax.experimental.pallas import tpu as pltpu

D_MODEL = 1024
H_A, KV_A, HD_A = 8, 2, 64
REP_A = H_A // KV_A
D_A = H_A * HD_A
IDX_H, IDX_D = 8, 64
TOPK_MAX = 256
H_B, D_NOPE, D_ROPE, V_B = 8, 64, 32, 64
Q_LORA, KV_LORA = 384, 256
D_B = H_B * V_B
D_MIX = D_A + D_B
ROPE_THETA = 10000.0
NEG_INF = -1e30
IN_SIZES = (D_A, KV_A * HD_A, KV_A * HD_A, IDX_H * IDX_D, IDX_D, IDX_H, Q_LORA, KV_LORA, D_ROPE, D_MIX)
IW_SCALE = IDX_H ** -0.5 * IDX_D ** -0.5
LOG2E = 1.4426950408889634
SCALE_B = (D_NOPE + D_ROPE) ** -0.5 * LOG2E
QB_PAD = 128
SLOPES = tuple(2.0 ** (-8.0 * (h + 1) / H_A) for h in range(H_A))

V_ROWS = 64
V_SLAB = 80

TM = 512
TQ = 256
TK = 256
CHAINS = 4
BISECT_FIXED = 22
BISECT_CAP = 320
VMEM_LIMIT = 48 * 1024 * 1024

_NT = (((1,), (1,)), ((), ()))
_TN = (((0,), (0,)), ((), ()))


def _rms_rows(v, g, eps=1e-6):
    return v * lax.rsqrt(jnp.mean(v * v, axis=-1, keepdims=True) + eps) * g


def _proj_kernel(x_ref, posr_ref, posc_ref, wqaT, wiqT, wzT, wvaT, wiwT, wka, wik, wcq, wckv, wkrp, wkrr,
                 gq, wuqT, gkv, wukp, wuvT, invr, invc,
                 qaT_o, ka_o, vaT_o, iqT_o, ik_o, iwT_o, qbT_o, kcat_o, vbT_o, szT_o):
    f32, bf = jnp.float32, jnp.bfloat16
    xb = x_ref[0].astype(bf)

    def nt(w):
        return lax.dot_general(w, xb, _NT, preferred_element_type=f32)

    def nn(w):
        return jnp.dot(xb, w, preferred_element_type=f32)

    qaT_o[0] = (nt(wqaT[...]) * LOG2E).astype(bf)
    iqT_o[0] = nt(wiqT[...]).astype(bf)
    iwT_o[0] = nt(wiwT[...]) * IW_SCALE
    ones_rows = jnp.where(lax.broadcasted_iota(jnp.int32, (V_SLAB - V_ROWS, TK), 0) == 0, 1.0, 0.0).astype(bf)
    va = nt(wvaT[...]).astype(bf)
    for c in range(TM // TK):
        for g in range(KV_A):
            vaT_o[0, c, g * V_SLAB:g * V_SLAB + V_ROWS] = va[g * HD_A:(g + 1) * HD_A, c * TK:(c + 1) * TK]
            vaT_o[0, c, g * V_SLAB + V_ROWS:(g + 1) * V_SLAB] = ones_rows
    z = nt(wzT[...])
    szT_o[0] = (z * jax.nn.sigmoid(z)).astype(bf)
    kk = nn(wka[...])
    for g in range(KV_A):
        ka_o[0, g] = kk[:, g * HD_A:(g + 1) * HD_A].astype(bf)
    ik_o[0] = nn(wik[...]).astype(bf)

    cqn = _rms_rows(nn(wcq[...]), gq[...]).astype(bf)
    qb = lax.dot_general(wuqT[...], cqn, _NT, preferred_element_type=f32)
    ang_t = invc[...] * posr_ref[0]
    cos_t, sin_t = jnp.cos(ang_t), jnp.sin(ang_t)
    half = D_ROPE // 2
    for h in range(H_B):
        b, o = h * (D_NOPE + D_ROPE), h * QB_PAD
        qbT_o[0, o:o + D_NOPE] = (qb[b:b + D_NOPE] * SCALE_B).astype(bf)
        x1 = qb[b + D_NOPE:b + D_NOPE + half]
        x2 = qb[b + D_NOPE + half:b + D_NOPE + D_ROPE]
        qbT_o[0, o + D_NOPE:o + D_NOPE + half] = ((x1 * cos_t - x2 * sin_t) * SCALE_B).astype(bf)
        qbT_o[0, o + D_NOPE + half:o + D_NOPE + D_ROPE] = ((x1 * sin_t + x2 * cos_t) * SCALE_B).astype(bf)
        qbT_o[0, o + D_NOPE + D_ROPE:o + QB_PAD] = jnp.zeros((QB_PAD - D_NOPE - D_ROPE, TM), bf)

    ckvn = _rms_rows(nn(wckv[...]), gkv[...]).astype(bf)
    knp = jnp.dot(ckvn, wukp[...], preferred_element_type=f32)
    ang = posc_ref[0] * invr[...]
    kro = nn(wkrp[...]) * jnp.cos(ang) + nn(wkrr[...]) * jnp.sin(ang)
    for h in range(H_B):
        kcat_o[0, h] = (knp[:, h * QB_PAD:(h + 1) * QB_PAD] + kro).astype(bf)
    vb = lax.dot_general(wuvT[...], ckvn, _NT, preferred_element_type=f32).astype(bf)
    for c in range(TM // TK):
        for h in range(H_B):
            vbT_o[0, c, h * V_SLAB:h * V_SLAB + V_ROWS] = vb[h * V_B:(h + 1) * V_B, c * TK:(c + 1) * TK]
            vbT_o[0, c, h * V_SLAB + V_ROWS:(h + 1) * V_SLAB] = ones_rows


def _qk_stage(nheads, score_fn, s_scr, tmax_scr):
    for h in range(nheads):
        s = score_fn(h)
        s_scr[h] = s
        tmax_scr[h:h + 1, :] = jnp.max(s, axis=0, keepdims=True)


def _softmax_stage(nheads, s_scr, tmax_scr, p_scr, m_scr):
    alphas = []
    for h in range(nheads):
        m_old = m_scr[h:h + 1, :]
        m_new = jnp.maximum(m_old, tmax_scr[h:h + 1, :])
        p_scr[h] = jnp.exp2(s_scr[h] - m_new).astype(jnp.bfloat16)
        alphas.append(jnp.exp2(m_old - m_new))
        m_scr[h:h + 1, :] = m_new
    return alphas


def _pv_stage(nheads, v_fn, alphas, p_scr, acc_scr):
    for h in range(nheads):
        acc_scr[h] = alphas[h] * acc_scr[h] + jnp.dot(v_fn(h), p_scr[h], preferred_element_type=jnp.float32)


def _finish(nheads, o_ref, acc_scr):
    for h in range(nheads):
        o_ref[0, h * V_ROWS:(h + 1) * V_ROWS, :] = acc_scr[h, :V_ROWS, :] / acc_scr[h, V_ROWS:V_ROWS + 1, :]


def _attn_a_kernel(iqT_ref, iwT_ref, ik_ref, qaT_ref, ka_ref, vaT_ref, posr_ref, posc_ref, o_ref,
                   isc_scr, bias_scr, st_scr, s_scr, tmax_scr, p_scr, m_scr, acc_scr, flag_smem, *, ktop):
    f32 = jnp.float32
    qi = pl.program_id(1)
    t_idx = qi * TQ + lax.broadcasted_iota(jnp.int32, (1, TQ), 1)
    iw = iwT_ref[0]

    def rows_of(j):
        return pl.ds(pl.multiple_of(j * TK, TK), TK)

    def idx_tile(j):
        k = ik_ref[0, rows_of(j), :]
        tot = jnp.zeros((TK, TQ), f32)
        for h in range(IDX_H):
            rel = jnp.dot(k, iqT_ref[0, h * IDX_D:(h + 1) * IDX_D, :], preferred_element_type=f32)
            tot = tot + jnp.maximum(rel, 0.0) * iw[h:h + 1, :]
        return tot

    def p1(j, amax):
        tot = idx_tile(j)
        isc_scr[rows_of(j), :] = tot
        return jnp.maximum(amax, jnp.max(jnp.abs(tot), axis=0, keepdims=True))

    amax = lax.fori_loop(0, qi, p1, jnp.zeros((1, TQ), f32))
    tot = idx_tile(qi)
    s_idx = qi * TK + lax.broadcasted_iota(jnp.int32, (TK, TQ), 0)
    causal = s_idx <= t_idx
    isc_scr[rows_of(qi), :] = jnp.where(causal, tot, NEG_INF)
    amax = jnp.maximum(amax, jnp.max(jnp.where(causal, jnp.abs(tot), 0.0), axis=0, keepdims=True))

    want = jnp.minimum(t_idx + 1, ktop).astype(f32)
    n_tiles = qi + 1

    def count_ge(mid):
        def body(j, acc):
            ind = jnp.where(isc_scr[rows_of(j), :] >= mid, 1.0, 0.0)
            return acc + ind.reshape(CHAINS, TK // (8 * CHAINS), 8, TQ).sum(axis=1)
        acc = lax.fori_loop(0, n_tiles, body, jnp.zeros((CHAINS, 8, TQ), f32))
        return jnp.sum(acc.reshape(CHAINS * 8, TQ), axis=0, keepdims=True)

    def bisect(_, st):
        lo, hi, clo, chi = st
        mid = 0.5 * (lo + hi)
        c = count_ge(mid)
        ge = c >= want
        return (jnp.where(ge, mid, lo), jnp.where(ge, hi, mid), jnp.where(ge, c, clo), jnp.where(ge, chi, c))

    st0 = (-amax, amax * 1.0001 + 1e-30, (t_idx + 1).astype(f32), jnp.zeros((1, TQ), f32))
    lo, hi, clo, chi = lax.fori_loop(0, BISECT_FIXED, bisect, st0)
    st_scr[0:1, :] = lo
    st_scr[1:2, :] = hi
    st_scr[2:3, :] = clo
    st_scr[3:4, :] = chi

    def unresolved_any(clo, chi, distinct):
        u = jnp.where((clo - chi) > (want - chi), distinct, 0.0)
        return (jnp.max(u) > 0.5).astype(jnp.int32)

    def refine_cond(c):
        it, more = c
        return jnp.logical_and(more > 0, it < BISECT_CAP)

    def refine(c):
        it, _ = c
        lo, hi, clo, chi = st_scr[0:1, :], st_scr[1:2, :], st_scr[2:3, :], st_scr[3:4, :]
        mid = 0.5 * (lo + hi)

        def body(j, carry):
            a8, mn8, mx8 = carry
            x = isc_scr[rows_of(j), :]
            ind = jnp.where(x >= mid, 1.0, 0.0)
            inside = jnp.where(x >= lo, jnp.where(x < hi, 1.0, 0.0), 0.0) > 0.5
            mn = jnp.where(inside, x, 3e38).reshape(TK // 8, 8, TQ).min(axis=0)
            mx = jnp.where(inside, x, -3e38).reshape(TK // 8, 8, TQ).max(axis=0)
            return (a8 + ind.reshape(TK // 8, 8, TQ).sum(axis=0), jnp.minimum(mn8, mn), jnp.maximum(mx8, mx))

        a8, mn8, mx8 = lax.fori_loop(
            0, n_tiles, body,
            (jnp.zeros((8, TQ), f32), jnp.full((8, TQ), 3e38, f32), jnp.full((8, TQ), -3e38, f32)))
        cnt = jnp.sum(a8, axis=0, keepdims=True)
        mn = jnp.min(mn8, axis=0, keepdims=True)
        mx = jnp.max(mx8, axis=0, keepdims=True)
        ge = cnt >= want
        lo2, hi2 = jnp.where(ge, mid, lo), jnp.where(ge, hi, mid)
        clo2, chi2 = jnp.where(ge, cnt, clo), jnp.where(ge, chi, cnt)
        st_scr[0:1, :] = lo2
        st_scr[1:2, :] = hi2
        st_scr[2:3, :] = clo2
        st_scr[3:4, :] = chi2
        return it + 1, unresolved_any(clo2, chi2, jnp.where(mx > mn, 1.0, 0.0))

    lax.while_loop(refine_cond, refine,
                   (jnp.int32(BISECT_FIXED), unresolved_any(clo, chi, jnp.ones((1, TQ), f32))))
    lo, hi, clo, chi = st_scr[0:1, :], st_scr[1:2, :], st_scr[2:3, :], st_scr[3:4, :]
    need = want - chi
    flag_smem[0] = unresolved_any(clo, chi, jnp.ones((1, TQ), f32))
    st_scr[4:5, :] = need
    st_scr[5:6, :] = jnp.zeros((1, TQ), f32)

    m_scr[...] = jnp.full(m_scr.shape, -3e38, f32)
    acc_scr[...] = jnp.zeros(acc_scr.shape, f32)
    pq = posr_ref[0]

    def select_bias(j):
        rows = rows_of(j)
        x = isc_scr[rows, :]
        nad = -jnp.abs(pq - posc_ref[0, rows, :])
        lo = st_scr[0:1, :]

        @pl.when(flag_smem[0] == 0)
        def _():
            bias_scr[...] = jnp.where(x >= lo, nad, NEG_INF)

        @pl.when(flag_smem[0] != 0)
        def _():
            hi, need, seen = st_scr[1:2, :], st_scr[4:5, :], st_scr[5:6, :]
            above = jnp.where(x >= hi, 1.0, 0.0)
            tied = jnp.where(x >= lo, 1.0, 0.0) - above
            r_i = lax.broadcasted_iota(jnp.int32, (TK, TK), 0)
            c_i = lax.broadcasted_iota(jnp.int32, (TK, TK), 1)
            tri = jnp.where(r_i >= c_i, 1.0, 0.0).astype(jnp.bfloat16)
            incl = jnp.dot(tri, tied.astype(jnp.bfloat16), preferred_element_type=f32)
            rank = incl - tied + seen
            keep = above + tied * jnp.where(rank < need, 1.0, 0.0)
            bias_scr[...] = jnp.where(keep > 0.5, nad, NEG_INF)
            st_scr[5:6, :] = seen + incl[TK - 1:TK, :]

    def qk(j):
        rows = rows_of(j)

        def score(h):
            qk_h = jnp.dot(ka_ref[0, h // REP_A, rows, :], qaT_ref[0, h * HD_A:(h + 1) * HD_A, :],
                           preferred_element_type=f32)
            return qk_h + (SLOPES[h] * LOG2E) * bias_scr[...]

        _qk_stage(H_A, score, s_scr, tmax_scr)

    def softmax():
        return _softmax_stage(H_A, s_scr, tmax_scr, p_scr, m_scr)

    def pv(j, alphas):
        _pv_stage(H_A, lambda h: vaT_ref[0, j, (h // REP_A) * V_SLAB:(h // REP_A + 1) * V_SLAB, :],
                  alphas, p_scr, acc_scr)

    select_bias(0)
    qk(0)

    def p3(j, _):
        select_bias(j + 1)
        alphas = softmax()
        qk(j + 1)
        pv(j, alphas)
        return 0

    lax.fori_loop(0, qi, p3, 0)
    pv(qi, softmax())
    _finish(H_A, o_ref, acc_scr)


def _attn_b_kernel(qbT_ref, kcat_ref, vbT_ref, o_ref, s_scr, tmax_scr, p_scr, m_scr, acc_scr):
    f32 = jnp.float32
    qi = pl.program_id(1)
    m_scr[...] = jnp.full(m_scr.shape, -3e38, f32)
    acc_scr[...] = jnp.zeros(acc_scr.shape, f32)

    def qk(j, masked):
        rows = pl.ds(pl.multiple_of(j * TK, TK), TK)
        if masked:
            s_idx = j * TK + lax.broadcasted_iota(jnp.int32, (TK, TQ), 0)
            t_idx = qi * TQ + lax.broadcasted_iota(jnp.int32, (TK, TQ), 1)
            causal = s_idx <= t_idx

        def score(h):
            s = jnp.dot(kcat_ref[0, h, rows, :], qbT_ref[0, h * QB_PAD:(h + 1) * QB_PAD, :],
                        preferred_element_type=f32)
            return jnp.where(causal, s, NEG_INF) if masked else s

        _qk_stage(H_B, score, s_scr, tmax_scr)

    def softmax():
        return _softmax_stage(H_B, s_scr, tmax_scr, p_scr, m_scr)

    def pv(j, alphas):
        _pv_stage(H_B, lambda h: vbT_ref[0, j, h * V_SLAB:(h + 1) * V_SLAB, :], alphas, p_scr, acc_scr)

    qk(qi, True)

    def body(i, _):
        alphas = softmax()
        qk(i, False)
        pv(jnp.where(i == 0, qi, i - 1), alphas)
        return 0

    lax.fori_loop(0, qi, body, 0)
    pv(jnp.maximum(qi - 1, 0), softmax())
    _finish(H_B, o_ref, acc_scr)


def _out_kernel(oaT_ref, obT_ref, szT_ref, x_ref, wout_ref, ga_ref, gb_ref, lng_ref, lnb_ref, o_ref, *, alpha):
    f32 = jnp.float32

    def rms_cols(v, g, eps=1e-6):
        return v * lax.rsqrt(jnp.mean(v * v, axis=0, keepdims=True) + eps) * g

    y_t = jnp.concatenate([rms_cols(oaT_ref[0], ga_ref[...]), rms_cols(obT_ref[0], gb_ref[...])], axis=0)
    y_t = (y_t * szT_ref[0].astype(f32)).astype(jnp.bfloat16)
    out = lax.dot_general(y_t, wout_ref[...], _TN, preferred_element_type=f32)
    hres = alpha * x_ref[0] + out
    mu = jnp.mean(hres, axis=-1, keepdims=True)
    d = hres - mu
    var = jnp.mean(d * d, axis=-1, keepdims=True)
    o_ref[0] = d * lax.rsqrt(var + 1e-5) * lng_ref[...] + lnb_ref[...]


def _full(shape):
    return pl.BlockSpec(shape, lambda *_: (0,) * len(shape))


def _layer(x, posr, posc, w_in, g_q_lat, w_uq, g_kv_lat, w_ukv, g_out_a, g_out_b, w_out, ln_g, ln_b, alpha):
    B, S, _ = x.shape
    f32, bf = jnp.float32, jnp.bfloat16
    assert S % TM == 0 and TM % TK == 0 and TQ == TK and HD_A == V_ROWS and V_B == V_ROWS
    ktop = min(TOPK_MAX, S // 4)
    n_kt = S // TK

    offs = np.concatenate([[0], np.cumsum(IN_SIZES)])
    wqa, wka, wva, wiq, wik, wiw, wcq, wckv, wkr, wz = [w_in[:, offs[i]:offs[i + 1]] for i in range(10)]
    half = D_ROPE // 2
    pad = jnp.zeros((D_MODEL, QB_PAD), f32)
    wkrp = pad.at[:, D_NOPE:D_NOPE + D_ROPE].set(wkr)
    wkrr = pad.at[:, D_NOPE:D_NOPE + half].set(-wkr[:, half:]).at[:, D_NOPE + half:D_NOPE + D_ROPE].set(wkr[:, :half])
    wukv3 = w_ukv.reshape(KV_LORA, H_B, D_NOPE + V_B)
    wukp = jnp.zeros((KV_LORA, H_B, QB_PAD), f32).at[:, :, :D_NOPE].set(wukv3[:, :, :D_NOPE]).reshape(KV_LORA, H_B * QB_PAD)
    wuvT = wukv3[:, :, D_NOPE:].reshape(KV_LORA, D_B).T
    inv_freq = ROPE_THETA ** (-jnp.arange(0, D_ROPE, 2, dtype=f32) / D_ROPE)
    invr = jnp.zeros((1, QB_PAD), f32).at[0, D_NOPE:D_NOPE + half].set(inv_freq).at[0, D_NOPE + half:D_NOPE + D_ROPE].set(inv_freq)
    invc = inv_freq.reshape(half, 1)
    weights = [
        (wqa * HD_A ** -0.5).T.astype(bf), wiq.T.astype(bf), wz.T.astype(bf), wva.T.astype(bf), wiw.T.astype(bf),
        wka.astype(bf), wik.astype(bf), wcq.astype(bf), wckv.astype(bf), wkrp.astype(bf), wkrr.astype(bf),
        g_q_lat.reshape(1, Q_LORA), w_uq.T.astype(bf), g_kv_lat.reshape(1, KV_LORA), wukp.astype(bf), wuvT.astype(bf),
        invr, invc,
    ]

    grid_m = (B, S // TM)
    tok = lambda b, i: (b, i, 0)
    col = lambda b, i: (b, 0, i)
    proj_out_shapes = [
        jax.ShapeDtypeStruct((B, D_A, S), bf),
        jax.ShapeDtypeStruct((B, KV_A, S, HD_A), bf),
        jax.ShapeDtypeStruct((B, n_kt, KV_A * V_SLAB, TK), bf),
        jax.ShapeDtypeStruct((B, IDX_H * IDX_D, S), bf),
        jax.ShapeDtypeStruct((B, S, IDX_D), bf),
        jax.ShapeDtypeStruct((B, IDX_H, S), f32),
        jax.ShapeDtypeStruct((B, H_B * QB_PAD, S), bf),
        jax.ShapeDtypeStruct((B, H_B, S, QB_PAD), bf),
        jax.ShapeDtypeStruct((B, n_kt, H_B * V_SLAB, TK), bf),
        jax.ShapeDtypeStruct((B, D_MIX, S), bf),
    ]
    proj_out_specs = [
        pl.BlockSpec((1, D_A, TM), col),
        pl.BlockSpec((1, KV_A, TM, HD_A), lambda b, i: (b, 0, i, 0)),
        pl.BlockSpec((1, TM // TK, KV_A * V_SLAB, TK), lambda b, i: (b, i, 0, 0)),
        pl.BlockSpec((1, IDX_H * IDX_D, TM), col),
        pl.BlockSpec((1, TM, IDX_D), tok),
        pl.BlockSpec((1, IDX_H, TM), col),
        pl.BlockSpec((1, H_B * QB_PAD, TM), col),
        pl.BlockSpec((1, H_B, TM, QB_PAD), lambda b, i: (b, 0, i, 0)),
        pl.BlockSpec((1, TM // TK, H_B * V_SLAB, TK), lambda b, i: (b, i, 0, 0)),
        pl.BlockSpec((1, D_MIX, TM), col),
    ]
    params = pltpu.CompilerParams(dimension_semantics=("parallel", "arbitrary"), vmem_limit_bytes=VMEM_LIMIT)
    qaT, ka, vaT, iqT, ik, iwT, qbT, kcat, vbT, szT = pl.pallas_call(
        _proj_kernel,
        grid=grid_m,
        in_specs=[pl.BlockSpec((1, TM, D_MODEL), tok), pl.BlockSpec((1, 1, TM), col), pl.BlockSpec((1, TM, 1), tok)]
        + [_full(w.shape) for w in weights],
        out_specs=proj_out_specs,
        out_shape=proj_out_shapes,
        compiler_params=params,
        name="proj",
    )(x, posr, posc, *weights)

    grid_q = (B, S // TQ)
    oaT = pl.pallas_call(
        functools.partial(_attn_a_kernel, ktop=ktop),
        grid=grid_q,
        in_specs=[
            pl.BlockSpec((1, IDX_H * IDX_D, TQ), col),
            pl.BlockSpec((1, IDX_H, TQ), col),
            pl.BlockSpec((1, S, IDX_D), lambda b, i: (b, 0, 0)),
            pl.BlockSpec((1, D_A, TQ), col),
            pl.BlockSpec((1, KV_A, S, HD_A), lambda b, i: (b, 0, 0, 0)),
            pl.BlockSpec((1, n_kt, KV_A * V_SLAB, TK), lambda b, i: (b, 0, 0, 0)),
            pl.BlockSpec((1, 1, TQ), col),
            pl.BlockSpec((1, S, 1), lambda b, i: (b, 0, 0)),
        ],
        out_specs=pl.BlockSpec((1, D_A, TQ), col),
        out_shape=jax.ShapeDtypeStruct((B, D_A, S), f32),
        scratch_shapes=[
            pltpu.VMEM((S, TQ), f32),
            pltpu.VMEM((TK, TQ), f32),
            pltpu.VMEM((8, TQ), f32),
            pltpu.VMEM((H_A, TK, TQ), f32),
            pltpu.VMEM((H_A, TQ), f32),
            pltpu.VMEM((H_A, TK, TQ), bf),
            pltpu.VMEM((H_A, TQ), f32),
            pltpu.VMEM((H_A, V_SLAB, TQ), f32),
            pltpu.SMEM((1,), jnp.int32),
        ],
        compiler_params=params,
        name="attn_a",
    )(iqT, iwT, ik, qaT, ka, vaT, posr, posc)

    obT = pl.pallas_call(
        _attn_b_kernel,
        grid=grid_q,
        in_specs=[
            pl.BlockSpec((1, H_B * QB_PAD, TQ), col),
            pl.BlockSpec((1, H_B, S, QB_PAD), lambda b, i: (b, 0, 0, 0)),
            pl.BlockSpec((1, n_kt, H_B * V_SLAB, TK), lambda b, i: (b, 0, 0, 0)),
        ],
        out_specs=pl.BlockSpec((1, D_B, TQ), col),
        out_shape=jax.ShapeDtypeStruct((B, D_B, S), f32),
        scratch_shapes=[
            pltpu.VMEM((H_B, TK, TQ), f32),
            pltpu.VMEM((H_B, TQ), f32),
            pltpu.VMEM((H_B, TK, TQ), bf),
            pltpu.VMEM((H_B, TQ), f32),
            pltpu.VMEM((H_B, V_SLAB, TQ), f32),
        ],
        compiler_params=params,
        name="attn_b",
    )(qbT, kcat, vbT)

    return pl.pallas_call(
        functools.partial(_out_kernel, alpha=alpha),
        grid=grid_m,
        in_specs=[
            pl.BlockSpec((1, D_A, TM), col),
            pl.BlockSpec((1, D_B, TM), col),
            pl.BlockSpec((1, D_MIX, TM), col),
            pl.BlockSpec((1, TM, D_MODEL), tok),
            _full((D_MIX, D_MODEL)),
            _full((D_A, 1)),
            _full((D_B, 1)),
            _full((1, D_MODEL)),
            _full((1, D_MODEL)),
        ],
        out_specs=pl.BlockSpec((1, TM, D_MODEL), tok),
        out_shape=jax.ShapeDtypeStruct((B, S, D_MODEL), f32),
        compiler_params=params,
        name="out",
    )(oaT, obT, szT, x, w_out.astype(bf), g_out_a.reshape(D_A, 1), g_out_b.reshape(D_B, 1),
      ln_g.reshape(1, D_MODEL), ln_b.reshape(1, D_MODEL))


def kernel(x, positions, w_in, g_q_lat, w_uq, g_kv_lat, w_ukv, g_out_a, g_out_b, w_out, ln_g, ln_b):
    depth = w_in.shape[0]
    alpha = (2.0 * depth) ** 0.25
    posf = positions.astype(jnp.float32)
    posr, posc = posf[:, None, :], posf[:, :, None]
    h = x
    for l in range(depth):
        h = _layer(h, posr, posc, w_in[l], g_q_lat[l], w_uq[l], g_kv_lat[l], w_ukv[l],
                   g_out_a[l], g_out_b[l], w_out[l], ln_g[l], ln_b[l], alpha)
    return h
```

```python
import functools

import numpy as np
import jax
import jax.numpy as jnp
from jax import lax
from jax.experimental import pallas as pl
from jax.experimental.pallas import tpu as pltpu

D_MODEL = 1024
H_A, KV_A, HD_A = 8, 2, 64
REP_A = H_A // KV_A
D_A = H_A * HD_A
IDX_H, IDX_D = 8, 64
TOPK_MAX = 256
H_B, D_NOPE, D_ROPE, V_B = 8, 64, 32, 64
Q_LORA, KV_LORA = 384, 256
D_B = H_B * V_B
D_MIX = D_A + D_B
ROPE_THETA = 10000.0
NEG_INF = -1e30
IN_SIZES = (D_A, KV_A * HD_A, KV_A * HD_A, IDX_H * IDX_D, IDX_D, IDX_H, Q_LORA, KV_LORA, D_ROPE, D_MIX)
IW_SCALE = IDX_H ** -0.5 * IDX_D ** -0.5
LOG2E = 1.4426950408889634
SCALE_B = (D_NOPE + D_ROPE) ** -0.5 * LOG2E
QB_PAD = 128
SLOPES = tuple(2.0 ** (-8.0 * (h + 1) / H_A) for h in range(H_A))

V_ROWS = 64
V_SLAB = 80

TM = 1024
TQ = 256
TK = 256
CHAINS = 4
BISECT_FIXED = 22
BISECT_CAP = 320
VMEM_LIMIT = 56 * 1024 * 1024

_NT = (((1,), (1,)), ((), ()))
_TN = (((0,), (0,)), ((), ()))


def _rms_rows(v, g, eps=1e-6):
    return v * lax.rsqrt(jnp.mean(v * v, axis=-1, keepdims=True) + eps) * g


def _proj_kernel(x_ref, posr_ref, wqaT, wiqT, wzT, wvaT, wiwT, wka, wik, wcq, wckv, wkrp, wkrr,
                 gq, wuqT, gkv, wukp, wuvT, invc,
                 qaT_o, ka_o, vaT_o, iqT_o, ik_o, iwT_o, qbT_o, kcat_o, vbT_o, szT_o):
    f32, bf = jnp.float32, jnp.bfloat16
    xb = x_ref[0].astype(bf)

    def nt(w):
        return lax.dot_general(w, xb, _NT, preferred_element_type=f32)

    def nn(w):
        return jnp.dot(xb, w, preferred_element_type=f32)

    qaT_o[0] = (nt(wqaT[...]) * LOG2E).astype(bf)
    iqT_o[0] = nt(wiqT[...]).astype(bf)
    iwT_o[0] = nt(wiwT[...]) * IW_SCALE
    ones_rows = jnp.where(lax.broadcasted_iota(jnp.int32, (V_SLAB - V_ROWS, TK), 0) == 0, 1.0, 0.0).astype(bf)
    va = nt(wvaT[...]).astype(bf)
    for c in range(TM // TK):
        for g in range(KV_A):
            vaT_o[0, c, g * V_SLAB:g * V_SLAB + V_ROWS] = va[g * HD_A:(g + 1) * HD_A, c * TK:(c + 1) * TK]
            vaT_o[0, c, g * V_SLAB + V_ROWS:(g + 1) * V_SLAB] = ones_rows
    z = nt(wzT[...])
    szT_o[0] = (z * jax.nn.sigmoid(z)).astype(bf)
    kk = nn(wka[...])
    for g in range(KV_A):
        ka_o[0, g] = kk[:, g * HD_A:(g + 1) * HD_A].astype(bf)
    ik_o[0] = nn(wik[...]).astype(bf)

    cqn = _rms_rows(nn(wcq[...]), gq[...]).astype(bf)
    qb = lax.dot_general(wuqT[...], cqn, _NT, preferred_element_type=f32)
    ang_t = invc[...] * posr_ref[0]
    cos_t, sin_t = jnp.cos(ang_t), jnp.sin(ang_t)
    half = D_ROPE // 2
    for h in range(H_B):
        b, o = h * (D_NOPE + D_ROPE), h * QB_PAD
        qbT_o[0, o:o + D_NOPE] = (qb[b:b + D_NOPE] * SCALE_B).astype(bf)
        x1 = qb[b + D_NOPE:b + D_NOPE + half]
        x2 = qb[b + D_NOPE + half:b + D_NOPE + D_ROPE]
        qbT_o[0, o + D_NOPE:o + D_NOPE + half] = ((x1 * cos_t - x2 * sin_t) * SCALE_B).astype(bf)
        qbT_o[0, o + D_NOPE + half:o + D_NOPE + D_ROPE] = ((x1 * sin_t + x2 * cos_t) * SCALE_B).astype(bf)
        qbT_o[0, o + D_NOPE + D_ROPE:o + QB_PAD] = jnp.zeros((QB_PAD - D_NOPE - D_ROPE, TM), bf)

    ckvn = _rms_rows(nn(wckv[...]), gkv[...]).astype(bf)
    knp = jnp.dot(ckvn, wukp[...], preferred_element_type=f32)
    cos_c = jnp.concatenate([cos_t, cos_t], axis=0).T
    sin_c = jnp.concatenate([sin_t, sin_t], axis=0).T
    kr_roped = (nn(wkrp[...]) * cos_c + nn(wkrr[...]) * sin_c).astype(bf)
    place = (lax.broadcasted_iota(jnp.int32, (D_ROPE, QB_PAD), 1)
             == lax.broadcasted_iota(jnp.int32, (D_ROPE, QB_PAD), 0) + D_NOPE)
    kro = jnp.dot(kr_roped, jnp.where(place, 1.0, 0.0).astype(bf), preferred_element_type=f32)
    for h in range(H_B):
        kcat_o[0, h] = (knp[:, h * QB_PAD:(h + 1) * QB_PAD] + kro).astype(bf)
    vb = lax.dot_general(wuvT[...], ckvn, _NT, preferred_element_type=f32).astype(bf)
    for c in range(TM // TK):
        for h in range(H_B):
            vbT_o[0, c, h * V_SLAB:h * V_SLAB + V_ROWS] = vb[h * V_B:(h + 1) * V_B, c * TK:(c + 1) * TK]
            vbT_o[0, c, h * V_SLAB + V_ROWS:(h + 1) * V_SLAB] = ones_rows


def _qk_stage(nheads, score_fn, s_scr, tmax_scr):
    for h in range(nheads):
        s = score_fn(h)
        s_scr[h] = s
        tmax_scr[h:h + 1, :] = jnp.max(s, axis=0, keepdims=True)


def _softmax_stage(nheads, s_scr, tmax_scr, p_scr, m_scr):
    alphas = []
    for h in range(nheads):
        m_old = m_scr[h:h + 1, :]
        m_new = jnp.maximum(m_old, tmax_scr[h:h + 1, :])
        p_scr[h] = jnp.exp2(s_scr[h] - m_new).astype(jnp.bfloat16)
        alphas.append(jnp.exp2(m_old - m_new))
        m_scr[h:h + 1, :] = m_new
    return alphas


def _pv_stage(nheads, v_fn, alphas, p_scr, acc_scr):
    for h in range(nheads):
        acc_scr[h] = alphas[h] * acc_scr[h] + jnp.dot(v_fn(h), p_scr[h], preferred_element_type=jnp.float32)


def _finish(nheads, o_ref, acc_scr):
    for h in range(nheads):
        o_ref[0, h * V_ROWS:(h + 1) * V_ROWS, :] = acc_scr[h, :V_ROWS, :] / acc_scr[h, V_ROWS:V_ROWS + 1, :]


def _attn_a_kernel(iqT_ref, iwT_ref, ik_ref, qaT_ref, ka_ref, vaT_ref, posr_ref, posc_ref, o_ref,
                   isc_scr, bias_scr, st_scr, s_scr, tmax_scr, p_scr, m_scr, acc_scr, flag_smem, *, ktop):
    f32 = jnp.float32
    qi = pl.program_id(1)
    t_idx = qi * TQ + lax.broadcasted_iota(jnp.int32, (1, TQ), 1)
    iw = iwT_ref[0]

    def rows_of(j):
        return pl.ds(pl.multiple_of(j * TK, TK), TK)

    def idx_tile(j):
        k = ik_ref[0, rows_of(j), :]
        tot = jnp.zeros((TK, TQ), f32)
        for h in range(IDX_H):
            rel = jnp.dot(k, iqT_ref[0, h * IDX_D:(h + 1) * IDX_D, :], preferred_element_type=f32)
            tot = tot + jnp.maximum(rel, 0.0) * iw[h:h + 1, :]
        return tot

    def p1(j, amax):
        tot = idx_tile(j)
        isc_scr[rows_of(j), :] = tot
        return jnp.maximum(amax, jnp.max(jnp.abs(tot), axis=0, keepdims=True))

    amax = lax.fori_loop(0, qi, p1, jnp.zeros((1, TQ), f32))
    tot = idx_tile(qi)
    s_idx = qi * TK + lax.broadcasted_iota(jnp.int32, (TK, TQ), 0)
    causal = s_idx <= t_idx
    isc_scr[rows_of(qi), :] = jnp.where(causal, tot, NEG_INF)
    amax = jnp.maximum(amax, jnp.max(jnp.where(causal, jnp.abs(tot), 0.0), axis=0, keepdims=True))

    want = jnp.minimum(t_idx + 1, ktop).astype(f32)
    n_tiles = qi + 1

    def count_ge(mid):
        def body(j, acc):
            ind = jnp.where(isc_scr[rows_of(j), :] >= mid, 1.0, 0.0)
            return acc + ind.reshape(CHAINS, TK // (8 * CHAINS), 8, TQ).sum(axis=1)
        acc = lax.fori_loop(0, n_tiles, body, jnp.zeros((CHAINS, 8, TQ), f32))
        return jnp.sum(acc.reshape(CHAINS * 8, TQ), axis=0, keepdims=True)

    def bisect(_, st):
        lo, hi, clo, chi = st
        mid = 0.5 * (lo + hi)
        c = count_ge(mid)
        ge = c >= want
        return (jnp.where(ge, mid, lo), jnp.where(ge, hi, mid), jnp.where(ge, c, clo), jnp.where(ge, chi, c))

    st0 = (-amax, amax * 1.0001 + 1e-30, (t_idx + 1).astype(f32), jnp.zeros((1, TQ), f32))
    lo, hi, clo, chi = lax.fori_loop(0, BISECT_FIXED, bisect, st0)
    st_scr[0:1, :] = lo
    st_scr[1:2, :] = hi
    st_scr[2:3, :] = clo
    st_scr[3:4, :] = chi

    def unresolved_any(clo, chi, distinct):
        u = jnp.where((clo - chi) > (want - chi), distinct, 0.0)
        return (jnp.max(u) > 0.5).astype(jnp.int32)

    def refine_cond(c):
        it, more = c
        return jnp.logical_and(more > 0, it < BISECT_CAP)

    def refine(c):
        it, _ = c
        lo, hi, clo, chi = st_scr[0:1, :], st_scr[1:2, :], st_scr[2:3, :], st_scr[3:4, :]
        mid = 0.5 * (lo + hi)

        def body(j, carry):
            a8, mn8, mx8 = carry
            x = isc_scr[rows_of(j), :]
            ind = jnp.where(x >= mid, 1.0, 0.0)
            mn = jnp.where(x >= lo, x, 3e38).reshape(TK // 8, 8, TQ).min(axis=0)
            mx = jnp.where(x < hi, x, -3e38).reshape(TK // 8, 8, TQ).max(axis=0)
            return (a8 + ind.reshape(TK // 8, 8, TQ).sum(axis=0), jnp.minimum(mn8, mn), jnp.maximum(mx8, mx))

        a8, mn8, mx8 = lax.fori_loop(
            0, n_tiles, body,
            (jnp.zeros((8, TQ), f32), jnp.full((8, TQ), 3e38, f32), jnp.full((8, TQ), -3e38, f32)))
        cnt = jnp.sum(a8, axis=0, keepdims=True)
        mn = jnp.min(mn8, axis=0, keepdims=True)
        mx = jnp.max(mx8, axis=0, keepdims=True)
        ge = cnt >= want
        lo2, hi2 = jnp.where(ge, mid, lo), jnp.where(ge, hi, mid)
        clo2, chi2 = jnp.where(ge, cnt, clo), jnp.where(ge, chi, cnt)
        st_scr[0:1, :] = lo2
        st_scr[1:2, :] = hi2
        st_scr[2:3, :] = clo2
        st_scr[3:4, :] = chi2
        return it + 1, unresolved_any(clo2, chi2, jnp.where(mx > mn, 1.0, 0.0))

    lax.while_loop(refine_cond, refine,
                   (jnp.int32(BISECT_FIXED), unresolved_any(clo, chi, jnp.ones((1, TQ), f32))))
    lo, hi, clo, chi = st_scr[0:1, :], st_scr[1:2, :], st_scr[2:3, :], st_scr[3:4, :]
    need = want - chi
    flag_smem[0] = unresolved_any(clo, chi, jnp.ones((1, TQ), f32))
    st_scr[4:5, :] = need
    st_scr[5:6, :] = jnp.zeros((1, TQ), f32)

    m_scr[...] = jnp.full(m_scr.shape, -3e38, f32)
    acc_scr[...] = jnp.zeros(acc_scr.shape, f32)
    pq = posr_ref[0]

    def select_bias(j):
        rows = rows_of(j)
        x = isc_scr[rows, :]
        nad = -jnp.abs(pq - posc_ref[0, rows, :])
        lo = st_scr[0:1, :]

        @pl.when(flag_smem[0] == 0)
        def _():
            bias_scr[...] = jnp.where(x >= lo, nad, NEG_INF)

        @pl.when(flag_smem[0] != 0)
        def _():
            hi, need, seen = st_scr[1:2, :], st_scr[4:5, :], st_scr[5:6, :]
            above = jnp.where(x >= hi, 1.0, 0.0)
            tied = jnp.where(x >= lo, 1.0, 0.0) - above
            r_i = lax.broadcasted_iota(jnp.int32, (TK, TK), 0)
            c_i = lax.broadcasted_iota(jnp.int32, (TK, TK), 1)
            tri = jnp.where(r_i >= c_i, 1.0, 0.0).astype(jnp.bfloat16)
            incl = jnp.dot(tri, tied.astype(jnp.bfloat16), preferred_element_type=f32)
            rank = incl - tied + seen
            keep = above + tied * jnp.where(rank < need, 1.0, 0.0)
            bias_scr[...] = jnp.where(keep > 0.5, nad, NEG_INF)
            st_scr[5:6, :] = seen + incl[TK - 1:TK, :]

    def qk(j):
        rows = rows_of(j)

        def score(h):
            qk_h = jnp.dot(ka_ref[0, h // REP_A, rows, :], qaT_ref[0, h * HD_A:(h + 1) * HD_A, :],
                           preferred_element_type=f32)
            return qk_h + (SLOPES[h] * LOG2E) * bias_scr[...]

        _qk_stage(H_A, score, s_scr, tmax_scr)

    def softmax():
        return _softmax_stage(H_A, s_scr, tmax_scr, p_scr, m_scr)

    def pv(j, alphas):
        _pv_stage(H_A, lambda h: vaT_ref[0, j, (h // REP_A) * V_SLAB:(h // REP_A + 1) * V_SLAB, :],
                  alphas, p_scr, acc_scr)

    select_bias(0)
    qk(0)

    def p3(j, _):
        select_bias(j + 1)
        alphas = softmax()
        qk(j + 1)
        pv(j, alphas)
        return 0

    lax.fori_loop(0, qi, p3, 0)
    pv(qi, softmax())
    _finish(H_A, o_ref, acc_scr)


def _attn_b_kernel(qbT_ref, kcat_ref, vbT_ref, o_ref, s_scr, tmax_scr, p_scr, m_scr, acc_scr):
    f32 = jnp.float32
    qi = pl.program_id(1)
    m_scr[...] = jnp.full(m_scr.shape, -3e38, f32)
    acc_scr[...] = jnp.zeros(acc_scr.shape, f32)

    def qk(j, masked):
        rows = pl.ds(pl.multiple_of(j * TK, TK), TK)
        if masked:
            s_idx = j * TK + lax.broadcasted_iota(jnp.int32, (TK, TQ), 0)
            t_idx = qi * TQ + lax.broadcasted_iota(jnp.int32, (TK, TQ), 1)
            causal = s_idx <= t_idx

        def score(h):
            s = jnp.dot(kcat_ref[0, h, rows, :], qbT_ref[0, h * QB_PAD:(h + 1) * QB_PAD, :],
                        preferred_element_type=f32)
            return jnp.where(causal, s, NEG_INF) if masked else s

        _qk_stage(H_B, score, s_scr, tmax_scr)

    def softmax():
        return _softmax_stage(H_B, s_scr, tmax_scr, p_scr, m_scr)

    def pv(j, alphas):
        _pv_stage(H_B, lambda h: vbT_ref[0, j, h * V_SLAB:(h + 1) * V_SLAB, :], alphas, p_scr, acc_scr)

    qk(qi, True)

    def body(i, _):
        alphas = softmax()
        qk(i, False)
        pv(jnp.where(i == 0, qi, i - 1), alphas)
        return 0

    lax.fori_loop(0, qi, body, 0)
    pv(jnp.maximum(qi - 1, 0), softmax())
    _finish(H_B, o_ref, acc_scr)


def _out_kernel(oaT_ref, obT_ref, szT_ref, x_ref, wout_ref, ga_ref, gb_ref, lng_ref, lnb_ref, o_ref, *, alpha):
    f32 = jnp.float32

    def rms_cols(v, g, eps=1e-6):
        return v * lax.rsqrt(jnp.mean(v * v, axis=0, keepdims=True) + eps) * g

    y_t = jnp.concatenate([rms_cols(oaT_ref[0], ga_ref[...]), rms_cols(obT_ref[0], gb_ref[...])], axis=0)
    y_t = (y_t * szT_ref[0].astype(f32)).astype(jnp.bfloat16)
    out = lax.dot_general(y_t, wout_ref[...], _TN, preferred_element_type=f32)
    hres = alpha * x_ref[0] + out
    mu = jnp.mean(hres, axis=-1, keepdims=True)
    d = hres - mu
    var = jnp.mean(d * d, axis=-1, keepdims=True)
    o_ref[0] = d * lax.rsqrt(var + 1e-5) * lng_ref[...] + lnb_ref[...]


def _full(shape):
    return pl.BlockSpec(shape, lambda *_: (0,) * len(shape))


def _layer(x, posr, posc, w_in, g_q_lat, w_uq, g_kv_lat, w_ukv, g_out_a, g_out_b, w_out, ln_g, ln_b, alpha):
    B, S, _ = x.shape
    f32, bf = jnp.float32, jnp.bfloat16
    assert S % TM == 0 and TM % TK == 0 and TQ == TK and HD_A == V_ROWS and V_B == V_ROWS
    ktop = min(TOPK_MAX, S // 4)
    n_kt = S // TK

    offs = np.concatenate([[0], np.cumsum(IN_SIZES)])
    wqa, wka, wva, wiq, wik, wiw, wcq, wckv, wkr, wz = [w_in[:, offs[i]:offs[i + 1]] for i in range(10)]
    half = D_ROPE // 2
    wkrp = wkr
    wkrr = jnp.concatenate([-wkr[:, half:], wkr[:, :half]], axis=1)
    wukv3 = w_ukv.reshape(KV_LORA, H_B, D_NOPE + V_B)
    wukp = jnp.zeros((KV_LORA, H_B, QB_PAD), f32).at[:, :, :D_NOPE].set(wukv3[:, :, :D_NOPE]).reshape(KV_LORA, H_B * QB_PAD)
    wuvT = wukv3[:, :, D_NOPE:].reshape(KV_LORA, D_B).T
    inv_freq = ROPE_THETA ** (-jnp.arange(0, D_ROPE, 2, dtype=f32) / D_ROPE)
    invc = inv_freq.reshape(half, 1)
    weights = [
        (wqa * HD_A ** -0.5).T.astype(bf), wiq.T.astype(bf), wz.T.astype(bf), wva.T.astype(bf), wiw.T.astype(bf),
        wka.astype(bf), wik.astype(bf), wcq.astype(bf), wckv.astype(bf), wkrp.astype(bf), wkrr.astype(bf),
        g_q_lat.reshape(1, Q_LORA), w_uq.T.astype(bf), g_kv_lat.reshape(1, KV_LORA), wukp.astype(bf), wuvT.astype(bf),
        invc,
    ]

    grid_m = (B, S // TM)
    tok = lambda b, i: (b, i, 0)
    col = lambda b, i: (b, 0, i)
    proj_out_shapes = [
        jax.ShapeDtypeStruct((B, D_A, S), bf),
        jax.ShapeDtypeStruct((B, KV_A, S, HD_A), bf),
        jax.ShapeDtypeStruct((B, n_kt, KV_A * V_SLAB, TK), bf),
        jax.ShapeDtypeStruct((B, IDX_H * IDX_D, S), bf),
        jax.ShapeDtypeStruct((B, S, IDX_D), bf),
        jax.ShapeDtypeStruct((B, IDX_H, S), f32),
        jax.ShapeDtypeStruct((B, H_B * QB_PAD, S), bf),
        jax.ShapeDtypeStruct((B, H_B, S, QB_PAD), bf),
        jax.ShapeDtypeStruct((B, n_kt, H_B * V_SLAB, TK), bf),
        jax.ShapeDtypeStruct((B, D_MIX, S), bf),
    ]
    proj_out_specs = [
        pl.BlockSpec((1, D_A, TM), col),
        pl.BlockSpec((1, KV_A, TM, HD_A), lambda b, i: (b, 0, i, 0)),
        pl.BlockSpec((1, TM // TK, KV_A * V_SLAB, TK), lambda b, i: (b, i, 0, 0)),
        pl.BlockSpec((1, IDX_H * IDX_D, TM), col),
        pl.BlockSpec((1, TM, IDX_D), tok),
        pl.BlockSpec((1, IDX_H, TM), col),
        pl.BlockSpec((1, H_B * QB_PAD, TM), col),
        pl.BlockSpec((1, H_B, TM, QB_PAD), lambda b, i: (b, 0, i, 0)),
        pl.BlockSpec((1, TM // TK, H_B * V_SLAB, TK), lambda b, i: (b, i, 0, 0)),
        pl.BlockSpec((1, D_MIX, TM), col),
    ]
    params = pltpu.CompilerParams(dimension_semantics=("parallel", "arbitrary"), vmem_limit_bytes=VMEM_LIMIT)
    qaT, ka, vaT, iqT, ik, iwT, qbT, kcat, vbT, szT = pl.pallas_call(
        _proj_kernel,
        grid=grid_m,
        in_specs=[pl.BlockSpec((1, TM, D_MODEL), tok), pl.BlockSpec((1, 1, TM), col)]
        + [_full(w.shape) for w in weights],
        out_specs=proj_out_specs,
        out_shape=proj_out_shapes,
        compiler_params=params,
        name="proj",
    )(x, posr, *weights)

    grid_q = (B, S // TQ)
    oaT = pl.pallas_call(
        functools.partial(_attn_a_kernel, ktop=ktop),
        grid=grid_q,
        in_specs=[
            pl.BlockSpec((1, IDX_H * IDX_D, TQ), col),
            pl.BlockSpec((1, IDX_H, TQ), col),
            pl.BlockSpec((1, S, IDX_D), lambda b, i: (b, 0, 0)),
            pl.BlockSpec((1, D_A, TQ), col),
            pl.BlockSpec((1, KV_A, S, HD_A), lambda b, i: (b, 0, 0, 0)),
            pl.BlockSpec((1, n_kt, KV_A * V_SLAB, TK), lambda b, i: (b, 0, 0, 0)),
            pl.BlockSpec((1, 1, TQ), col),
            pl.BlockSpec((1, S, 1), lambda b, i: (b, 0, 0)),
        ],
        out_specs=pl.BlockSpec((1, D_A, TQ), col),
        out_shape=jax.ShapeDtypeStruct((B, D_A, S), f32),
        scratch_shapes=[
            pltpu.VMEM((S, TQ), f32),
            pltpu.VMEM((TK, TQ), f32),
            pltpu.VMEM((8, TQ), f32),
            pltpu.VMEM((H_A, TK, TQ), f32),
            pltpu.VMEM((H_A, TQ), f32),
            pltpu.VMEM((H_A, TK, TQ), bf),
            pltpu.VMEM((H_A, TQ), f32),
            pltpu.VMEM((H_A, V_SLAB, TQ), f32),
            pltpu.SMEM((1,), jnp.int32),
        ],
        compiler_params=params,
        name="attn_a",
    )(iqT, iwT, ik, qaT, ka, vaT, posr, posc)

    obT = pl.pallas_call(
        _attn_b_kernel,
        grid=grid_q,
        in_specs=[
            pl.BlockSpec((1, H_B * QB_PAD, TQ), col),
            pl.BlockSpec((1, H_B, S, QB_PAD), lambda b, i: (b, 0, 0, 0)),
            pl.BlockSpec((1, n_kt, H_B * V_SLAB, TK), lambda b, i: (b, 0, 0, 0)),
        ],
        out_specs=pl.BlockSpec((1, D_B, TQ), col),
        out_shape=jax.ShapeDtypeStruct((B, D_B, S), f32),
        scratch_shapes=[
            pltpu.VMEM((H_B, TK, TQ), f32),
            pltpu.VMEM((H_B, TQ), f32),
            pltpu.VMEM((H_B, TK, TQ), bf),
            pltpu.VMEM((H_B, TQ), f32),
            pltpu.VMEM((H_B, V_SLAB, TQ), f32),
        ],
        compiler_params=params,
        name="attn_b",
    )(qbT, kcat, vbT)

    return pl.pallas_call(
        functools.partial(_out_kernel, alpha=alpha),
        grid=grid_m,
        in_specs=[
            pl.BlockSpec((1, D_A, TM), col),
            pl.BlockSpec((1, D_B, TM), col),
            pl.BlockSpec((1, D_MIX, TM), col),
            pl.BlockSpec((1, TM, D_MODEL), tok),
            _full((D_MIX, D_MODEL)),
            _full((D_A, 1)),
            _full((D_B, 1)),
            _full((1, D_MODEL)),
            _full((1, D_MODEL)),
        ],
        out_specs=pl.BlockSpec((1, TM, D_MODEL), tok),
        out_shape=jax.ShapeDtypeStruct((B, S, D_MODEL), f32),
        compiler_params=params,
        name="out",
    )(oaT, obT, szT, x, w_out.astype(bf), g_out_a.reshape(D_A, 1), g_out_b.reshape(D_B, 1),
      ln_g.reshape(1, D_MODEL), ln_b.reshape(1, D_MODEL))


def kernel(x, positions, w_in, g_q_lat, w_uq, g_kv_lat, w_ukv, g_out_a, g_out_b, w_out, ln_g, ln_b):
    depth = w_in.shape[0]
    alpha = (2.0 * depth) ** 0.25
    posf = positions.astype(jnp.float32)
    posr, posc = posf[:, None, :], posf[:, :, None]
    h = x
    for l in range(depth):
        h = _layer(h, posr, posc, w_in[l], g_q_lat[l], w_uq[l], g_kv_lat[l], w_ukv[l],
                   g_out_a[l], g_out_b[l], w_out[l], ln_g[l], ln_b[l], alpha)
    return h
```

```python
import functools

import numpy as np
import jax
import jax.numpy as jnp
from jax import lax
from jax.experimental import pallas as pl
from jax.experimental.pallas import tpu as pltpu

D_MODEL = 1024
H_A, KV_A, HD_A = 8, 2, 64
REP_A = H_A // KV_A
D_A = H_A * HD_A
IDX_H, IDX_D = 8, 64
TOPK_MAX = 256
H_B, D_NOPE, D_ROPE, V_B = 8, 64, 32, 64
Q_LORA, KV_LORA = 384, 256
D_B = H_B * V_B
D_MIX = D_A + D_B
ROPE_THETA = 10000.0
NEG_INF = -1e30
IN_SIZES = (D_A, KV_A * HD_A, KV_A * HD_A, IDX_H * IDX_D, IDX_D, IDX_H, Q_LORA, KV_LORA, D_ROPE, D_MIX)
IW_SCALE = IDX_H ** -0.5 * IDX_D ** -0.5
LOG2E = 1.4426950408889634
SCALE_B = (D_NOPE + D_ROPE) ** -0.5 * LOG2E
QB_PAD = 128
SLOPES = tuple(2.0 ** (-8.0 * (h + 1) / H_A) for h in range(H_A))

V_ROWS = 64
V_SLAB = 80

TM = 1024
TQ = 256
TK = 256
CHAINS = 4
BISECT_FIXED = 22
BISECT_CAP = 320
VMEM_LIMIT = 56 * 1024 * 1024

_NT = (((1,), (1,)), ((), ()))
_TN = (((0,), (0,)), ((), ()))


def _rms_rows(v, g, eps=1e-6):
    return v * lax.rsqrt(jnp.mean(v * v, axis=-1, keepdims=True) + eps) * g


def _proj_kernel(x_ref, posr_ref, wqaT, wiqT, wzT, wvaT, wiwT, wka, wik, wcq, wckv, wkrp, wkrr,
                 gq, wuqT, gkv, wukp, wuvT, invc,
                 qaT_o, ka_o, vaT_o, iqT_o, ik_o, iwT_o, qbT_o, kcat_o, vbT_o, szT_o):
    f32, bf = jnp.float32, jnp.bfloat16
    xb = x_ref[0].astype(bf)

    def nt(w):
        return lax.dot_general(w, xb, _NT, preferred_element_type=f32)

    def nn(w):
        return jnp.dot(xb, w, preferred_element_type=f32)

    qaT_o[0] = (nt(wqaT[...]) * LOG2E).astype(bf)
    iqT_o[0] = nt(wiqT[...]).astype(bf)
    iwT_o[0] = nt(wiwT[...]) * IW_SCALE
    ones_rows = jnp.where(lax.broadcasted_iota(jnp.int32, (V_SLAB - V_ROWS, TK), 0) == 0, 1.0, 0.0).astype(bf)
    va = nt(wvaT[...]).astype(bf)
    for c in range(TM // TK):
        for g in range(KV_A):
            vaT_o[0, c, g * V_SLAB:g * V_SLAB + V_ROWS] = va[g * HD_A:(g + 1) * HD_A, c * TK:(c + 1) * TK]
            vaT_o[0, c, g * V_SLAB + V_ROWS:(g + 1) * V_SLAB] = ones_rows
    z = nt(wzT[...])
    szT_o[0] = (z * jax.nn.sigmoid(z)).astype(bf)
    kk = nn(wka[...])
    for g in range(KV_A):
        ka_o[0, g] = kk[:, g * HD_A:(g + 1) * HD_A].astype(bf)
    ik_o[0] = nn(wik[...]).astype(bf)

    cqn = _rms_rows(nn(wcq[...]), gq[...]).astype(bf)
    qb = lax.dot_general(wuqT[...], cqn, _NT, preferred_element_type=f32)
    ang_t = invc[...] * posr_ref[0]
    cos_t, sin_t = jnp.cos(ang_t), jnp.sin(ang_t)
    half = D_ROPE // 2
    for h in range(H_B):
        b, o = h * (D_NOPE + D_ROPE), h * QB_PAD
        qbT_o[0, o:o + D_NOPE] = (qb[b:b + D_NOPE] * SCALE_B).astype(bf)
        x1 = qb[b + D_NOPE:b + D_NOPE + half]
        x2 = qb[b + D_NOPE + half:b + D_NOPE + D_ROPE]
        qbT_o[0, o + D_NOPE:o + D_NOPE + half] = ((x1 * cos_t - x2 * sin_t) * SCALE_B).astype(bf)
        qbT_o[0, o + D_NOPE + half:o + D_NOPE + D_ROPE] = ((x1 * sin_t + x2 * cos_t) * SCALE_B).astype(bf)
        qbT_o[0, o + D_NOPE + D_ROPE:o + QB_PAD] = jnp.zeros((QB_PAD - D_NOPE - D_ROPE, TM), bf)

    ckvn = _rms_rows(nn(wckv[...]), gkv[...]).astype(bf)
    knp = jnp.dot(ckvn, wukp[...], preferred_element_type=f32)
    cos_c = jnp.concatenate([cos_t, cos_t], axis=0).T
    sin_c = jnp.concatenate([sin_t, sin_t], axis=0).T
    kr_roped = (nn(wkrp[...]) * cos_c + nn(wkrr[...]) * sin_c).astype(bf)
    place = (lax.broadcasted_iota(jnp.int32, (D_ROPE, QB_PAD), 1)
             == lax.broadcasted_iota(jnp.int32, (D_ROPE, QB_PAD), 0) + D_NOPE)
    kro = jnp.dot(kr_roped, jnp.where(place, 1.0, 0.0).astype(bf), preferred_element_type=f32)
    for h in range(H_B):
        kcat_o[0, h] = (knp[:, h * QB_PAD:(h + 1) * QB_PAD] + kro).astype(bf)
    vb = lax.dot_general(wuvT[...], ckvn, _NT, preferred_element_type=f32).astype(bf)
    for c in range(TM // TK):
        for h in range(H_B):
            vbT_o[0, c, h * V_SLAB:h * V_SLAB + V_ROWS] = vb[h * V_B:(h + 1) * V_B, c * TK:(c + 1) * TK]
            vbT_o[0, c, h * V_SLAB + V_ROWS:(h + 1) * V_SLAB] = ones_rows


def _qk_stage(nheads, score_fn, s_scr, tmax_scr):
    for h in range(nheads):
        s = score_fn(h)
        s_scr[h] = s
        tmax_scr[h:h + 1, :] = jnp.max(s, axis=0, keepdims=True)


def _softmax_stage(nheads, s_scr, tmax_scr, p_scr, m_scr):
    alphas = []
    for h in range(nheads):
        m_old = m_scr[h:h + 1, :]
        m_new = jnp.maximum(m_old, tmax_scr[h:h + 1, :])
        p_scr[h] = jnp.exp2(s_scr[h] - m_new).astype(jnp.bfloat16)
        alphas.append(jnp.exp2(m_old - m_new))
        m_scr[h:h + 1, :] = m_new
    return alphas


def _pv_stage(nheads, v_fn, alphas, p_scr, acc_scr):
    for h in range(nheads):
        acc_scr[h] = alphas[h] * acc_scr[h] + jnp.dot(v_fn(h), p_scr[h], preferred_element_type=jnp.float32)


def _finish(nheads, o_ref, acc_scr):
    for h in range(nheads):
        o_ref[0, h * V_ROWS:(h + 1) * V_ROWS, :] = acc_scr[h, :V_ROWS, :] / acc_scr[h, V_ROWS:V_ROWS + 1, :]


def _attn_a_kernel(iqT_ref, iwT_ref, ik_ref, qaT_ref, ka_ref, vaT_ref, posr_ref, posc_ref, o_ref,
                   isc_scr, bias_scr, st_scr, s_scr, tmax_scr, p_scr, m_scr, acc_scr, flag_smem, *, ktop):
    f32 = jnp.float32
    qi = pl.program_id(1)
    t_idx = qi * TQ + lax.broadcasted_iota(jnp.int32, (1, TQ), 1)
    iw = iwT_ref[0]

    def rows_of(j):
        return pl.ds(pl.multiple_of(j * TK, TK), TK)

    def idx_tile(j):
        k = ik_ref[0, rows_of(j), :]
        tot = jnp.zeros((TK, TQ), f32)
        for h in range(IDX_H):
            rel = jnp.dot(k, iqT_ref[0, h * IDX_D:(h + 1) * IDX_D, :], preferred_element_type=f32)
            tot = tot + jnp.maximum(rel, 0.0) * iw[h:h + 1, :]
        return tot

    def p1(j, amax):
        tot = idx_tile(j)
        isc_scr[rows_of(j), :] = tot
        return jnp.maximum(amax, jnp.max(jnp.abs(tot), axis=0, keepdims=True))

    amax = lax.fori_loop(0, qi // 2, lambda c, a: p1(2 * c + 1, p1(2 * c, a)), jnp.zeros((1, TQ), f32))
    st_scr[6:7, :] = amax

    @pl.when(lax.rem(qi, 2) == 1)
    def _():
        st_scr[6:7, :] = p1(qi - 1, st_scr[6:7, :])

    amax = st_scr[6:7, :]
    tot = idx_tile(qi)
    s_idx = qi * TK + lax.broadcasted_iota(jnp.int32, (TK, TQ), 0)
    causal = s_idx <= t_idx
    isc_scr[rows_of(qi), :] = jnp.where(causal, tot, NEG_INF)
    amax = jnp.maximum(amax, jnp.max(jnp.where(causal, jnp.abs(tot), 0.0), axis=0, keepdims=True))

    want = jnp.minimum(t_idx + 1, ktop).astype(f32)
    n_tiles = qi + 1

    def count_ge(mid):
        def body(j, acc):
            ind = jnp.where(isc_scr[rows_of(j), :] >= mid, 1.0, 0.0)
            return acc + ind.reshape(CHAINS, TK // (8 * CHAINS), 8, TQ).sum(axis=1)
        acc = lax.fori_loop(0, n_tiles, body, jnp.zeros((CHAINS, 8, TQ), f32))
        return jnp.sum(acc.reshape(CHAINS * 8, TQ), axis=0, keepdims=True)

    def bisect(_, st):
        lo, hi, clo, chi = st
        mid = 0.5 * (lo + hi)
        c = count_ge(mid)
        ge = c >= want
        return (jnp.where(ge, mid, lo), jnp.where(ge, hi, mid), jnp.where(ge, c, clo), jnp.where(ge, chi, c))

    st0 = (-amax, amax * 1.0001 + 1e-30, (t_idx + 1).astype(f32), jnp.zeros((1, TQ), f32))
    lo, hi, clo, chi = lax.fori_loop(0, BISECT_FIXED, bisect, st0)
    st_scr[0:1, :] = lo
    st_scr[1:2, :] = hi
    st_scr[2:3, :] = clo
    st_scr[3:4, :] = chi

    def unresolved_any(clo, chi, distinct):
        u = jnp.where((clo - chi) > (want - chi), distinct, 0.0)
        return (jnp.max(u) > 0.5).astype(jnp.int32)

    def refine_cond(c):
        it, more = c
        return jnp.logical_and(more > 0, it < BISECT_CAP)

    def refine(c):
        it, _ = c
        lo, hi, clo, chi = st_scr[0:1, :], st_scr[1:2, :], st_scr[2:3, :], st_scr[3:4, :]
        mid = 0.5 * (lo + hi)

        def body(j, carry):
            a8, mn8, mx8 = carry
            x = isc_scr[rows_of(j), :]
            ind = jnp.where(x >= mid, 1.0, 0.0)
            mn = jnp.where(x >= lo, x, 3e38).reshape(TK // 8, 8, TQ).min(axis=0)
            mx = jnp.where(x < hi, x, -3e38).reshape(TK // 8, 8, TQ).max(axis=0)
            return (a8 + ind.reshape(TK // 8, 8, TQ).sum(axis=0), jnp.minimum(mn8, mn), jnp.maximum(mx8, mx))

        a8, mn8, mx8 = lax.fori_loop(
            0, n_tiles, body,
            (jnp.zeros((8, TQ), f32), jnp.full((8, TQ), 3e38, f32), jnp.full((8, TQ), -3e38, f32)))
        cnt = jnp.sum(a8, axis=0, keepdims=True)
        mn = jnp.min(mn8, axis=0, keepdims=True)
        mx = jnp.max(mx8, axis=0, keepdims=True)
        ge = cnt >= want
        lo2, hi2 = jnp.where(ge, mid, lo), jnp.where(ge, hi, mid)
        clo2, chi2 = jnp.where(ge, cnt, clo), jnp.where(ge, chi, cnt)
        st_scr[0:1, :] = lo2
        st_scr[1:2, :] = hi2
        st_scr[2:3, :] = clo2
        st_scr[3:4, :] = chi2
        return it + 1, unresolved_any(clo2, chi2, jnp.where(mx > mn, 1.0, 0.0))

    lax.while_loop(refine_cond, refine,
                   (jnp.int32(BISECT_FIXED), unresolved_any(clo, chi, jnp.ones((1, TQ), f32))))
    lo, hi, clo, chi = st_scr[0:1, :], st_scr[1:2, :], st_scr[2:3, :], st_scr[3:4, :]
    need = want - chi
    flag_smem[0] = unresolved_any(clo, chi, jnp.ones((1, TQ), f32))
    st_scr[4:5, :] = need
    st_scr[5:6, :] = jnp.zeros((1, TQ), f32)

    m_scr[...] = jnp.full(m_scr.shape, -3e38, f32)
    acc_scr[...] = jnp.zeros(acc_scr.shape, f32)
    pq = posr_ref[0]

    def select_bias(j):
        rows = rows_of(j)
        x = isc_scr[rows, :]
        nad = -jnp.abs(pq - posc_ref[0, rows, :])
        lo = st_scr[0:1, :]

        @pl.when(flag_smem[0] == 0)
        def _():
            bias_scr[...] = jnp.where(x >= lo, nad, NEG_INF)

        @pl.when(flag_smem[0] != 0)
        def _():
            hi, need, seen = st_scr[1:2, :], st_scr[4:5, :], st_scr[5:6, :]
            above = jnp.where(x >= hi, 1.0, 0.0)
            tied = jnp.where(x >= lo, 1.0, 0.0) - above
            r_i = lax.broadcasted_iota(jnp.int32, (TK, TK), 0)
            c_i = lax.broadcasted_iota(jnp.int32, (TK, TK), 1)
            tri = jnp.where(r_i >= c_i, 1.0, 0.0).astype(jnp.bfloat16)
            incl = jnp.dot(tri, tied.astype(jnp.bfloat16), preferred_element_type=f32)
            rank = incl - tied + seen
            keep = above + tied * jnp.where(rank < need, 1.0, 0.0)
            bias_scr[...] = jnp.where(keep > 0.5, nad, NEG_INF)
            st_scr[5:6, :] = seen + incl[TK - 1:TK, :]

    def qk(j):
        rows = rows_of(j)

        def score(h):
            qk_h = jnp.dot(ka_ref[0, h // REP_A, rows, :], qaT_ref[0, h * HD_A:(h + 1) * HD_A, :],
                           preferred_element_type=f32)
            return qk_h + (SLOPES[h] * LOG2E) * bias_scr[...]

        _qk_stage(H_A, score, s_scr, tmax_scr)

    def softmax():
        return _softmax_stage(H_A, s_scr, tmax_scr, p_scr, m_scr)

    def pv(j, alphas):
        _pv_stage(H_A, lambda h: vaT_ref[0, j, (h // REP_A) * V_SLAB:(h // REP_A + 1) * V_SLAB, :],
                  alphas, p_scr, acc_scr)

    select_bias(0)
    qk(0)

    def p3(j, _):
        select_bias(j + 1)
        alphas = softmax()
        qk(j + 1)
        pv(j, alphas)
        return 0

    lax.fori_loop(0, qi, p3, 0)
    pv(qi, softmax())
    _finish(H_A, o_ref, acc_scr)


def _attn_b_kernel(qbT_ref, kcat_ref, vbT_ref, o_ref, s_scr, tmax_scr, p_scr, m_scr, acc_scr):
    f32 = jnp.float32
    qi = pl.program_id(1)
    m_scr[...] = jnp.full(m_scr.shape, -3e38, f32)
    acc_scr[...] = jnp.zeros(acc_scr.shape, f32)

    def qk(j, masked):
        rows = pl.ds(pl.multiple_of(j * TK, TK), TK)
        if masked:
            s_idx = j * TK + lax.broadcasted_iota(jnp.int32, (TK, TQ), 0)
            t_idx = qi * TQ + lax.broadcasted_iota(jnp.int32, (TK, TQ), 1)
            causal = s_idx <= t_idx

        def score(h):
            s = jnp.dot(kcat_ref[0, h, rows, :], qbT_ref[0, h * QB_PAD:(h + 1) * QB_PAD, :],
                        preferred_element_type=f32)
            return jnp.where(causal, s, NEG_INF) if masked else s

        _qk_stage(H_B, score, s_scr, tmax_scr)

    def softmax():
        return _softmax_stage(H_B, s_scr, tmax_scr, p_scr, m_scr)

    def pv(j, alphas):
        _pv_stage(H_B, lambda h: vbT_ref[0, j, h * V_SLAB:(h + 1) * V_SLAB, :], alphas, p_scr, acc_scr)

    qk(qi, True)

    def body(i, _):
        alphas = softmax()
        qk(i, False)
        pv(jnp.where(i == 0, qi, i - 1), alphas)
        return 0

    lax.fori_loop(0, qi, body, 0)
    pv(jnp.maximum(qi - 1, 0), softmax())
    _finish(H_B, o_ref, acc_scr)


def _out_kernel(oaT_ref, obT_ref, szT_ref, x_ref, wout_ref, ga_ref, gb_ref, lng_ref, lnb_ref, o_ref, *, alpha):
    f32 = jnp.float32

    def rms_cols(v, g, eps=1e-6):
        return v * lax.rsqrt(jnp.mean(v * v, axis=0, keepdims=True) + eps) * g

    y_t = jnp.concatenate([rms_cols(oaT_ref[0], ga_ref[...]), rms_cols(obT_ref[0], gb_ref[...])], axis=0)
    y_t = (y_t * szT_ref[0].astype(f32)).astype(jnp.bfloat16)
    out = lax.dot_general(y_t, wout_ref[...], _TN, preferred_element_type=f32)
    hres = alpha * x_ref[0] + out
    mu = jnp.mean(hres, axis=-1, keepdims=True)
    d = hres - mu
    var = jnp.mean(d * d, axis=-1, keepdims=True)
    o_ref[0] = d * lax.rsqrt(var + 1e-5) * lng_ref[...] + lnb_ref[...]


def _full(shape):
    return pl.BlockSpec(shape, lambda *_: (0,) * len(shape))


def _layer(x, posr, posc, w_in, g_q_lat, w_uq, g_kv_lat, w_ukv, g_out_a, g_out_b, w_out, ln_g, ln_b, alpha):
    B, S, _ = x.shape
    f32, bf = jnp.float32, jnp.bfloat16
    assert S % TM == 0 and TM % TK == 0 and TQ == TK and HD_A == V_ROWS and V_B == V_ROWS
    ktop = min(TOPK_MAX, S // 4)
    n_kt = S // TK

    offs = np.concatenate([[0], np.cumsum(IN_SIZES)])
    wqa, wka, wva, wiq, wik, wiw, wcq, wckv, wkr, wz = [w_in[:, offs[i]:offs[i + 1]] for i in range(10)]
    half = D_ROPE // 2
    wkrp = wkr
    wkrr = jnp.concatenate([-wkr[:, half:], wkr[:, :half]], axis=1)
    wukv3 = w_ukv.reshape(KV_LORA, H_B, D_NOPE + V_B)
    wukp = jnp.zeros((KV_LORA, H_B, QB_PAD), f32).at[:, :, :D_NOPE].set(wukv3[:, :, :D_NOPE]).reshape(KV_LORA, H_B * QB_PAD)
    wuvT = wukv3[:, :, D_NOPE:].reshape(KV_LORA, D_B).T
    inv_freq = ROPE_THETA ** (-jnp.arange(0, D_ROPE, 2, dtype=f32) / D_ROPE)
    invc = inv_freq.reshape(half, 1)
    weights = [
        (wqa * HD_A ** -0.5).T.astype(bf), wiq.T.astype(bf), wz.T.astype(bf), wva.T.astype(bf), wiw.T.astype(bf),
        wka.astype(bf), wik.astype(bf), wcq.astype(bf), wckv.astype(bf), wkrp.astype(bf), wkrr.astype(bf),
        g_q_lat.reshape(1, Q_LORA), w_uq.T.astype(bf), g_kv_lat.reshape(1, KV_LORA), wukp.astype(bf), wuvT.astype(bf),
        invc,
    ]

    grid_m = (B, S // TM)
    tok = lambda b, i: (b, i, 0)
    col = lambda b, i: (b, 0, i)
    proj_out_shapes = [
        jax.ShapeDtypeStruct((B, D_A, S), bf),
        jax.ShapeDtypeStruct((B, KV_A, S, HD_A), bf),
        jax.ShapeDtypeStruct((B, n_kt, KV_A * V_SLAB, TK), bf),
        jax.ShapeDtypeStruct((B, IDX_H * IDX_D, S), bf),
        jax.ShapeDtypeStruct((B, S, IDX_D), bf),
        jax.ShapeDtypeStruct((B, IDX_H, S), f32),
        jax.ShapeDtypeStruct((B, H_B * QB_PAD, S), bf),
        jax.ShapeDtypeStruct((B, H_B, S, QB_PAD), bf),
        jax.ShapeDtypeStruct((B, n_kt, H_B * V_SLAB, TK), bf),
        jax.ShapeDtypeStruct((B, D_MIX, S), bf),
    ]
    proj_out_specs = [
        pl.BlockSpec((1, D_A, TM), col),
        pl.BlockSpec((1, KV_A, TM, HD_A), lambda b, i: (b, 0, i, 0)),
        pl.BlockSpec((1, TM // TK, KV_A * V_SLAB, TK), lambda b, i: (b, i, 0, 0)),
        pl.BlockSpec((1, IDX_H * IDX_D, TM), col),
        pl.BlockSpec((1, TM, IDX_D), tok),
        pl.BlockSpec((1, IDX_H, TM), col),
        pl.BlockSpec((1, H_B * QB_PAD, TM), col),
        pl.BlockSpec((1, H_B, TM, QB_PAD), lambda b, i: (b, 0, i, 0)),
        pl.BlockSpec((1, TM // TK, H_B * V_SLAB, TK), lambda b, i: (b, i, 0, 0)),
        pl.BlockSpec((1, D_MIX, TM), col),
    ]
    params = pltpu.CompilerParams(dimension_semantics=("parallel", "arbitrary"), vmem_limit_bytes=VMEM_LIMIT)
    qaT, ka, vaT, iqT, ik, iwT, qbT, kcat, vbT, szT = pl.pallas_call(
        _proj_kernel,
        grid=grid_m,
        in_specs=[pl.BlockSpec((1, TM, D_MODEL), tok), pl.BlockSpec((1, 1, TM), col)]
        + [_full(w.shape) for w in weights],
        out_specs=proj_out_specs,
        out_shape=proj_out_shapes,
        compiler_params=params,
        name="proj",
    )(x, posr, *weights)

    grid_q = (B, S // TQ)
    oaT = pl.pallas_call(
        functools.partial(_attn_a_kernel, ktop=ktop),
        grid=grid_q,
        in_specs=[
            pl.BlockSpec((1, IDX_H * IDX_D, TQ), col),
            pl.BlockSpec((1, IDX_H, TQ), col),
            pl.BlockSpec((1, S, IDX_D), lambda b, i: (b, 0, 0)),
            pl.BlockSpec((1, D_A, TQ), col),
            pl.BlockSpec((1, KV_A, S, HD_A), lambda b, i: (b, 0, 0, 0)),
            pl.BlockSpec((1, n_kt, KV_A * V_SLAB, TK), lambda b, i: (b, 0, 0, 0)),
            pl.BlockSpec((1, 1, TQ), col),
            pl.BlockSpec((1, S, 1), lambda b, i: (b, 0, 0)),
        ],
        out_specs=pl.BlockSpec((1, D_A, TQ), col),
        out_shape=jax.ShapeDtypeStruct((B, D_A, S), f32),
        scratch_shapes=[
            pltpu.VMEM((S, TQ), f32),
            pltpu.VMEM((TK, TQ), f32),
            pltpu.VMEM((8, TQ), f32),
            pltpu.VMEM((H_A, TK, TQ), f32),
            pltpu.VMEM((H_A, TQ), f32),
            pltpu.VMEM((H_A, TK, TQ), bf),
            pltpu.VMEM((H_A, TQ), f32),
            pltpu.VMEM((H_A, V_SLAB, TQ), f32),
            pltpu.SMEM((1,), jnp.int32),
        ],
        compiler_params=params,
        name="attn_a",
    )(iqT, iwT, ik, qaT, ka, vaT, posr, posc)

    obT = pl.pallas_call(
        _attn_b_kernel,
        grid=grid_q,
        in_specs=[
            pl.BlockSpec((1, H_B * QB_PAD, TQ), col),
            pl.BlockSpec((1, H_B, S, QB_PAD), lambda b, i: (b, 0, 0, 0)),
            pl.BlockSpec((1, n_kt, H_B * V_SLAB, TK), lambda b, i: (b, 0, 0, 0)),
        ],
        out_specs=pl.BlockSpec((1, D_B, TQ), col),
        out_shape=jax.ShapeDtypeStruct((B, D_B, S), f32),
        scratch_shapes=[
            pltpu.VMEM((H_B, TK, TQ), f32),
            pltpu.VMEM((H_B, TQ), f32),
            pltpu.VMEM((H_B, TK, TQ), bf),
            pltpu.VMEM((H_B, TQ), f32),
            pltpu.VMEM((H_B, V_SLAB, TQ), f32),
        ],
        compiler_params=params,
        name="attn_b",
    )(qbT, kcat, vbT)

    return pl.pallas_call(
        functools.partial(_out_kernel, alpha=alpha),
        grid=grid_m,
        in_specs=[
            pl.BlockSpec((1, D_A, TM), col),
            pl.BlockSpec((1, D_B, TM), col),
            pl.BlockSpec((1, D_MIX, TM), col),
            pl.BlockSpec((1, TM, D_MODEL), tok),
            _full((D_MIX, D_MODEL)),
            _full((D_A, 1)),
            _full((D_B, 1)),
            _full((1, D_MODEL)),
            _full((1, D_MODEL)),
        ],
        out_specs=pl.BlockSpec((1, TM, D_MODEL), tok),
        out_shape=jax.ShapeDtypeStruct((B, S, D_MODEL), f32),
        compiler_params=params,
        name="out",
    )(oaT, obT, szT, x, w_out.astype(bf), g_out_a.reshape(D_A, 1), g_out_b.reshape(D_B, 1),
      ln_g.reshape(1, D_MODEL), ln_b.reshape(1, D_MODEL))


def kernel(x, positions, w_in, g_q_lat, w_uq, g_kv_lat, w_ukv, g_out_a, g_out_b, w_out, ln_g, ln_b):
    depth = w_in.shape[0]
    alpha = (2.0 * depth) ** 0.25
    posf = positions.astype(jnp.float32)
    posr, posc = posf[:, None, :], posf[:, :, None]
    h = x
    for l in range(depth):
        h = _layer(h, posr, posc, w_in[l], g_q_lat[l], w_uq[l], g_kv_lat[l], w_ukv[l],
                   g_out_a[l], g_out_b[l], w_out[l], ln_g[l], ln_b[l], alpha)
    return h
```

```python
import functools

import numpy as np
import jax
import jax.numpy as jnp
from jax import lax
from jax.experimental import pallas as pl
from jax.experimental.pallas import tpu as pltpu

D_MODEL = 1024
H_A, KV_A, HD_A = 8, 2, 64
REP_A = H_A // KV_A
D_A = H_A * HD_A
IDX_H, IDX_D = 8, 64
TOPK_MAX = 256
H_B, D_NOPE, D_ROPE, V_B = 8, 64, 32, 64
Q_LORA, KV_LORA = 384, 256
D_B = H_B * V_B
D_MIX = D_A + D_B
ROPE_THETA = 10000.0
NEG_INF = -1e30
IN_SIZES = (D_A, KV_A * HD_A, KV_A * HD_A, IDX_H * IDX_D, IDX_D, IDX_H, Q_LORA, KV_LORA, D_ROPE, D_MIX)
IW_SCALE = IDX_H ** -0.5 * IDX_D ** -0.5
LOG2E = 1.4426950408889634
SCALE_B = (D_NOPE + D_ROPE) ** -0.5 * LOG2E
LANES, SUBLANES = 128, 8
BF16_ROWS = 2 * SUBLANES
QB_PAD = LANES
BIG = 3e38
SLOPES = tuple(2.0 ** (-8.0 * (h + 1) / H_A) for h in range(H_A))

V_ROWS = 64
V_SLAB = V_ROWS + BF16_ROWS

TM = 1024
TQ = 256
TK = 256
CHAINS = 4
BISECT_FIXED = 22
BISECT_CAP = 320
VMEM_LIMIT = 56 * 1024 * 1024

_NT = (((1,), (1,)), ((), ()))
_TN = (((0,), (0,)), ((), ()))


def _rms_rows(v, g, eps=1e-6):
    return v * lax.rsqrt(jnp.mean(v * v, axis=-1, keepdims=True) + eps) * g


def _proj_kernel(x_ref, posr_ref, wqaT, wiqT, wzT, wvaT, wiwT, wka, wik, wcq, wckv, wkrp, wkrr,
                 gq, wuqT, gkv, wukp, wuvT, invc,
                 qaT_o, ka_o, vaT_o, iqT_o, ik_o, iwT_o, qbT_o, kcat_o, vbT_o, szT_o):
    f32, bf = jnp.float32, jnp.bfloat16
    xb = x_ref[0].astype(bf)

    def nt(w):
        return lax.dot_general(w, xb, _NT, preferred_element_type=f32)

    def nn(w):
        return jnp.dot(xb, w, preferred_element_type=f32)

    qaT_o[0] = (nt(wqaT[...]) * LOG2E).astype(bf)
    iqT_o[0] = nt(wiqT[...]).astype(bf)
    iwT_o[0] = nt(wiwT[...]) * IW_SCALE
    ones_rows = jnp.where(lax.broadcasted_iota(jnp.int32, (V_SLAB - V_ROWS, TK), 0) == 0, 1.0, 0.0).astype(bf)
    va = nt(wvaT[...]).astype(bf)
    for c in range(TM // TK):
        for g in range(KV_A):
            vaT_o[0, c, g * V_SLAB:g * V_SLAB + V_ROWS] = va[g * HD_A:(g + 1) * HD_A, c * TK:(c + 1) * TK]
            vaT_o[0, c, g * V_SLAB + V_ROWS:(g + 1) * V_SLAB] = ones_rows
    z = nt(wzT[...])
    szT_o[0] = (z * jax.nn.sigmoid(z)).astype(bf)
    kk = nn(wka[...])
    for g in range(KV_A):
        ka_o[0, g] = kk[:, g * HD_A:(g + 1) * HD_A].astype(bf)
    ik_o[0] = nn(wik[...]).astype(bf)

    cqn = _rms_rows(nn(wcq[...]), gq[...]).astype(bf)
    qb = lax.dot_general(wuqT[...], cqn, _NT, preferred_element_type=f32)
    ang_t = invc[...] * posr_ref[0]
    cos_t, sin_t = jnp.cos(ang_t), jnp.sin(ang_t)
    half = D_ROPE // 2
    for h in range(H_B):
        b, o = h * (D_NOPE + D_ROPE), h * QB_PAD
        qbT_o[0, o:o + D_NOPE] = (qb[b:b + D_NOPE] * SCALE_B).astype(bf)
        x1 = qb[b + D_NOPE:b + D_NOPE + half]
        x2 = qb[b + D_NOPE + half:b + D_NOPE + D_ROPE]
        qbT_o[0, o + D_NOPE:o + D_NOPE + half] = ((x1 * cos_t - x2 * sin_t) * SCALE_B).astype(bf)
        qbT_o[0, o + D_NOPE + half:o + D_NOPE + D_ROPE] = ((x1 * sin_t + x2 * cos_t) * SCALE_B).astype(bf)
        qbT_o[0, o + D_NOPE + D_ROPE:o + QB_PAD] = jnp.zeros((QB_PAD - D_NOPE - D_ROPE, TM), bf)

    ckvn = _rms_rows(nn(wckv[...]), gkv[...]).astype(bf)
    knp = jnp.dot(ckvn, wukp[...], preferred_element_type=f32)
    cos_c = jnp.concatenate([cos_t, cos_t], axis=0).T
    sin_c = jnp.concatenate([sin_t, sin_t], axis=0).T
    kr_roped = (nn(wkrp[...]) * cos_c + nn(wkrr[...]) * sin_c).astype(bf)
    place = (lax.broadcasted_iota(jnp.int32, (D_ROPE, QB_PAD), 1)
             == lax.broadcasted_iota(jnp.int32, (D_ROPE, QB_PAD), 0) + D_NOPE)
    kro = jnp.dot(kr_roped, jnp.where(place, 1.0, 0.0).astype(bf), preferred_element_type=f32)
    for h in range(H_B):
        kcat_o[0, h] = (knp[:, h * QB_PAD:(h + 1) * QB_PAD] + kro).astype(bf)
    vb = lax.dot_general(wuvT[...], ckvn, _NT, preferred_element_type=f32).astype(bf)
    for c in range(TM // TK):
        for h in range(H_B):
            vbT_o[0, c, h * V_SLAB:h * V_SLAB + V_ROWS] = vb[h * V_B:(h + 1) * V_B, c * TK:(c + 1) * TK]
            vbT_o[0, c, h * V_SLAB + V_ROWS:(h + 1) * V_SLAB] = ones_rows


def _qk_stage(nheads, score_fn, s_scr, tmax_scr):
    for h in range(nheads):
        s = score_fn(h)
        s_scr[h] = s
        tmax_scr[h:h + 1, :] = jnp.max(s, axis=0, keepdims=True)


def _softmax_stage(nheads, s_scr, tmax_scr, p_scr, m_scr):
    alphas = []
    for h in range(nheads):
        m_old = m_scr[h:h + 1, :]
        m_new = jnp.maximum(m_old, tmax_scr[h:h + 1, :])
        p_scr[h] = jnp.exp2(s_scr[h] - m_new).astype(jnp.bfloat16)
        alphas.append(jnp.exp2(m_old - m_new))
        m_scr[h:h + 1, :] = m_new
    return alphas


def _pv_stage(nheads, v_fn, alphas, p_scr, acc_scr):
    for h in range(nheads):
        acc_scr[h] = alphas[h] * acc_scr[h] + jnp.dot(v_fn(h), p_scr[h], preferred_element_type=jnp.float32)


def _finish(nheads, o_ref, acc_scr):
    for h in range(nheads):
        o_ref[0, h * V_ROWS:(h + 1) * V_ROWS, :] = acc_scr[h, :V_ROWS, :] / acc_scr[h, V_ROWS:V_ROWS + 1, :]


def _attn_a_kernel(iqT_ref, iwT_ref, ik_ref, qaT_ref, ka_ref, vaT_ref, posr_ref, posc_ref, o_ref,
                   isc_scr, bias_scr, st_scr, s_scr, tmax_scr, p_scr, m_scr, acc_scr, flag_smem, *, ktop):
    f32 = jnp.float32
    qi = pl.program_id(1)
    t_idx = qi * TQ + lax.broadcasted_iota(jnp.int32, (1, TQ), 1)
    iw = iwT_ref[0]

    def rows_of(j):
        return pl.ds(pl.multiple_of(j * TK, TK), TK)

    def idx_tile(j):
        k = ik_ref[0, rows_of(j), :]
        tot = jnp.zeros((TK, TQ), f32)
        for h in range(IDX_H):
            rel = jnp.dot(k, iqT_ref[0, h * IDX_D:(h + 1) * IDX_D, :], preferred_element_type=f32)
            tot = tot + jnp.maximum(rel, 0.0) * iw[h:h + 1, :]
        return tot

    def p1(j, amax):
        tot = idx_tile(j)
        isc_scr[rows_of(j), :] = tot
        return jnp.maximum(amax, jnp.max(jnp.abs(tot), axis=0, keepdims=True))

    amax = lax.fori_loop(0, qi // 2, lambda c, a: p1(2 * c + 1, p1(2 * c, a)), jnp.zeros((1, TQ), f32))
    st_scr[6:7, :] = amax

    @pl.when(lax.rem(qi, 2) == 1)
    def _():
        st_scr[6:7, :] = p1(qi - 1, st_scr[6:7, :])

    amax = st_scr[6:7, :]
    tot = idx_tile(qi)
    s_idx = qi * TK + lax.broadcasted_iota(jnp.int32, (TK, TQ), 0)
    causal = s_idx <= t_idx
    isc_scr[rows_of(qi), :] = jnp.where(causal, tot, NEG_INF)
    amax = jnp.maximum(amax, jnp.max(jnp.where(causal, jnp.abs(tot), 0.0), axis=0, keepdims=True))

    want = jnp.minimum(t_idx + 1, ktop).astype(f32)
    n_tiles = qi + 1

    def count_ge(mid):
        def body(j, acc):
            ind = jnp.where(isc_scr[rows_of(j), :] >= mid, 1.0, 0.0)
            return acc + ind.reshape(CHAINS, TK // (SUBLANES * CHAINS), SUBLANES, TQ).sum(axis=1)
        acc = lax.fori_loop(0, n_tiles, body, jnp.zeros((CHAINS, SUBLANES, TQ), f32))
        return jnp.sum(acc.reshape(CHAINS * SUBLANES, TQ), axis=0, keepdims=True)

    def bisect(_, st):
        lo, hi, clo, chi = st
        mid = 0.5 * (lo + hi)
        c = count_ge(mid)
        ge = c >= want
        return (jnp.where(ge, mid, lo), jnp.where(ge, hi, mid), jnp.where(ge, c, clo), jnp.where(ge, chi, c))

    st0 = (-amax, amax * 1.0001 + 1e-30, (t_idx + 1).astype(f32), jnp.zeros((1, TQ), f32))
    lo, hi, clo, chi = lax.fori_loop(0, BISECT_FIXED, bisect, st0)
    st_scr[0:1, :] = lo
    st_scr[1:2, :] = hi
    st_scr[2:3, :] = clo
    st_scr[3:4, :] = chi

    def unresolved_any(clo, chi, distinct):
        u = jnp.where((clo - chi) > (want - chi), distinct, 0.0)
        return (jnp.max(u) > 0.5).astype(jnp.int32)

    def refine_cond(c):
        it, more = c
        return jnp.logical_and(more > 0, it < BISECT_CAP)

    def refine(c):
        it, _ = c
        lo, hi, clo, chi = st_scr[0:1, :], st_scr[1:2, :], st_scr[2:3, :], st_scr[3:4, :]
        mid = 0.5 * (lo + hi)

        def body(j, carry):
            a8, mn8, mx8 = carry
            x = isc_scr[rows_of(j), :]
            ind = jnp.where(x >= mid, 1.0, 0.0)
            mn = jnp.where(x >= lo, x, BIG).reshape(TK // SUBLANES, SUBLANES, TQ).min(axis=0)
            mx = jnp.where(x < hi, x, -BIG).reshape(TK // SUBLANES, SUBLANES, TQ).max(axis=0)
            return (a8 + ind.reshape(TK // SUBLANES, SUBLANES, TQ).sum(axis=0), jnp.minimum(mn8, mn), jnp.maximum(mx8, mx))

        a8, mn8, mx8 = lax.fori_loop(
            0, n_tiles, body,
            (jnp.zeros((SUBLANES, TQ), f32), jnp.full((SUBLANES, TQ), BIG, f32), jnp.full((SUBLANES, TQ), -BIG, f32)))
        cnt = jnp.sum(a8, axis=0, keepdims=True)
        mn = jnp.min(mn8, axis=0, keepdims=True)
        mx = jnp.max(mx8, axis=0, keepdims=True)
        ge = cnt >= want
        lo2, hi2 = jnp.where(ge, mid, lo), jnp.where(ge, hi, mid)
        clo2, chi2 = jnp.where(ge, cnt, clo), jnp.where(ge, chi, cnt)
        st_scr[0:1, :] = lo2
        st_scr[1:2, :] = hi2
        st_scr[2:3, :] = clo2
        st_scr[3:4, :] = chi2
        return it + 1, unresolved_any(clo2, chi2, jnp.where(mx > mn, 1.0, 0.0))

    lax.while_loop(refine_cond, refine,
                   (jnp.int32(BISECT_FIXED), unresolved_any(clo, chi, jnp.ones((1, TQ), f32))))
    lo, hi, clo, chi = st_scr[0:1, :], st_scr[1:2, :], st_scr[2:3, :], st_scr[3:4, :]
    need = want - chi
    flag_smem[0] = unresolved_any(clo, chi, jnp.ones((1, TQ), f32))
    st_scr[4:5, :] = need
    st_scr[5:6, :] = jnp.zeros((1, TQ), f32)

    m_scr[...] = jnp.full(m_scr.shape, -BIG, f32)
    acc_scr[...] = jnp.zeros(acc_scr.shape, f32)
    pq = posr_ref[0]

    def select_bias(j):
        rows = rows_of(j)
        x = isc_scr[rows, :]
        nad = -jnp.abs(pq - posc_ref[0, rows, :])
        lo = st_scr[0:1, :]

        @pl.when(flag_smem[0] == 0)
        def _():
            bias_scr[...] = jnp.where(x >= lo, nad, NEG_INF)

        @pl.when(flag_smem[0] != 0)
        def _():
            hi, need, seen = st_scr[1:2, :], st_scr[4:5, :], st_scr[5:6, :]
            above = jnp.where(x >= hi, 1.0, 0.0)
            tied = jnp.where(x >= lo, 1.0, 0.0) - above
            r_i = lax.broadcasted_iota(jnp.int32, (TK, TK), 0)
            c_i = lax.broadcasted_iota(jnp.int32, (TK, TK), 1)
            tri = jnp.where(r_i >= c_i, 1.0, 0.0).astype(jnp.bfloat16)
            incl = jnp.dot(tri, tied.astype(jnp.bfloat16), preferred_element_type=f32)
            rank = incl - tied + seen
            keep = above + tied * jnp.where(rank < need, 1.0, 0.0)
            bias_scr[...] = jnp.where(keep > 0.5, nad, NEG_INF)
            st_scr[5:6, :] = seen + incl[TK - 1:TK, :]

    def qk(j):
        rows = rows_of(j)

        def score(h):
            qk_h = jnp.dot(ka_ref[0, h // REP_A, rows, :], qaT_ref[0, h * HD_A:(h + 1) * HD_A, :],
                           preferred_element_type=f32)
            return qk_h + (SLOPES[h] * LOG2E) * bias_scr[...]

        _qk_stage(H_A, score, s_scr, tmax_scr)

    def softmax():
        return _softmax_stage(H_A, s_scr, tmax_scr, p_scr, m_scr)

    def pv(j, alphas):
        _pv_stage(H_A, lambda h: vaT_ref[0, j, (h // REP_A) * V_SLAB:(h // REP_A + 1) * V_SLAB, :],
                  alphas, p_scr, acc_scr)

    select_bias(0)
    qk(0)

    def p3(j, _):
        select_bias(j + 1)
        alphas = softmax()
        qk(j + 1)
        pv(j, alphas)
        return 0

    lax.fori_loop(0, qi, p3, 0)
    pv(qi, softmax())
    _finish(H_A, o_ref, acc_scr)


def _attn_b_kernel(qbT_ref, kcat_ref, vbT_ref, o_ref, s_scr, tmax_scr, p_scr, m_scr, acc_scr):
    f32 = jnp.float32
    qi = pl.program_id(1)
    m_scr[...] = jnp.full(m_scr.shape, -BIG, f32)
    acc_scr[...] = jnp.zeros(acc_scr.shape, f32)

    def qk(j, masked):
        rows = pl.ds(pl.multiple_of(j * TK, TK), TK)
        if masked:
            s_idx = j * TK + lax.broadcasted_iota(jnp.int32, (TK, TQ), 0)
            t_idx = qi * TQ + lax.broadcasted_iota(jnp.int32, (TK, TQ), 1)
            causal = s_idx <= t_idx

        def score(h):
            s = jnp.dot(kcat_ref[0, h, rows, :], qbT_ref[0, h * QB_PAD:(h + 1) * QB_PAD, :],
                        preferred_element_type=f32)
            return jnp.where(causal, s, NEG_INF) if masked else s

        _qk_stage(H_B, score, s_scr, tmax_scr)

    def softmax():
        return _softmax_stage(H_B, s_scr, tmax_scr, p_scr, m_scr)

    def pv(j, alphas):
        _pv_stage(H_B, lambda h: vbT_ref[0, j, h * V_SLAB:(h + 1) * V_SLAB, :], alphas, p_scr, acc_scr)

    qk(qi, True)

    def body(i, _):
        alphas = softmax()
        qk(i, False)
        pv(jnp.where(i == 0, qi, i - 1), alphas)
        return 0

    lax.fori_loop(0, qi, body, 0)
    pv(jnp.maximum(qi - 1, 0), softmax())
    _finish(H_B, o_ref, acc_scr)


def _out_kernel(oaT_ref, obT_ref, szT_ref, x_ref, wout_ref, ga_ref, gb_ref, lng_ref, lnb_ref, o_ref, *, alpha):
    f32 = jnp.float32

    def rms_cols(v, g, eps=1e-6):
        return v * lax.rsqrt(jnp.mean(v * v, axis=0, keepdims=True) + eps) * g

    y_t = jnp.concatenate([rms_cols(oaT_ref[0], ga_ref[...]), rms_cols(obT_ref[0], gb_ref[...])], axis=0)
    y_t = (y_t * szT_ref[0].astype(f32)).astype(jnp.bfloat16)
    out = lax.dot_general(y_t, wout_ref[...], _TN, preferred_element_type=f32)
    hres = alpha * x_ref[0] + out
    mu = jnp.mean(hres, axis=-1, keepdims=True)
    d = hres - mu
    var = jnp.mean(d * d, axis=-1, keepdims=True)
    o_ref[0] = d * lax.rsqrt(var + 1e-5) * lng_ref[...] + lnb_ref[...]


def _full(shape):
    return pl.BlockSpec(shape, lambda *_: (0,) * len(shape))


def _layer(x, posr, posc, w_in, g_q_lat, w_uq, g_kv_lat, w_ukv, g_out_a, g_out_b, w_out, ln_g, ln_b, alpha):
    B, S, _ = x.shape
    f32, bf = jnp.float32, jnp.bfloat16
    assert S % TM == 0 and TM % TK == 0 and TQ == TK and HD_A == V_ROWS and V_B == V_ROWS
    ktop = min(TOPK_MAX, S // 4)
    n_kt = S // TK

    offs = np.concatenate([[0], np.cumsum(IN_SIZES)])
    wqa, wka, wva, wiq, wik, wiw, wcq, wckv, wkr, wz = [w_in[:, offs[i]:offs[i + 1]] for i in range(10)]
    half = D_ROPE // 2
    wkrp = wkr
    wkrr = jnp.concatenate([-wkr[:, half:], wkr[:, :half]], axis=1)
    wukv3 = w_ukv.reshape(KV_LORA, H_B, D_NOPE + V_B)
    wukp = jnp.zeros((KV_LORA, H_B, QB_PAD), f32).at[:, :, :D_NOPE].set(wukv3[:, :, :D_NOPE]).reshape(KV_LORA, H_B * QB_PAD)
    wuvT = wukv3[:, :, D_NOPE:].reshape(KV_LORA, D_B).T
    inv_freq = ROPE_THETA ** (-jnp.arange(0, D_ROPE, 2, dtype=f32) / D_ROPE)
    invc = inv_freq.reshape(half, 1)
    weights = [
        (wqa * HD_A ** -0.5).T.astype(bf), wiq.T.astype(bf), wz.T.astype(bf), wva.T.astype(bf), wiw.T.astype(bf),
        wka.astype(bf), wik.astype(bf), wcq.astype(bf), wckv.astype(bf), wkrp.astype(bf), wkrr.astype(bf),
        g_q_lat.reshape(1, Q_LORA), w_uq.T.astype(bf), g_kv_lat.reshape(1, KV_LORA), wukp.astype(bf), wuvT.astype(bf),
        invc,
    ]

    grid_m = (B, S // TM)
    tok = lambda b, i: (b, i, 0)
    col = lambda b, i: (b, 0, i)
    proj_out_shapes = [
        jax.ShapeDtypeStruct((B, D_A, S), bf),
        jax.ShapeDtypeStruct((B, KV_A, S, HD_A), bf),
        jax.ShapeDtypeStruct((B, n_kt, KV_A * V_SLAB, TK), bf),
        jax.ShapeDtypeStruct((B, IDX_H * IDX_D, S), bf),
        jax.ShapeDtypeStruct((B, S, IDX_D), bf),
        jax.ShapeDtypeStruct((B, IDX_H, S), f32),
        jax.ShapeDtypeStruct((B, H_B * QB_PAD, S), bf),
        jax.ShapeDtypeStruct((B, H_B, S, QB_PAD), bf),
        jax.ShapeDtypeStruct((B, n_kt, H_B * V_SLAB, TK), bf),
        jax.ShapeDtypeStruct((B, D_MIX, S), bf),
    ]
    proj_out_specs = [
        pl.BlockSpec((1, D_A, TM), col),
        pl.BlockSpec((1, KV_A, TM, HD_A), lambda b, i: (b, 0, i, 0)),
        pl.BlockSpec((1, TM // TK, KV_A * V_SLAB, TK), lambda b, i: (b, i, 0, 0)),
        pl.BlockSpec((1, IDX_H * IDX_D, TM), col),
        pl.BlockSpec((1, TM, IDX_D), tok),
        pl.BlockSpec((1, IDX_H, TM), col),
        pl.BlockSpec((1, H_B * QB_PAD, TM), col),
        pl.BlockSpec((1, H_B, TM, QB_PAD), lambda b, i: (b, 0, i, 0)),
        pl.BlockSpec((1, TM // TK, H_B * V_SLAB, TK), lambda b, i: (b, i, 0, 0)),
        pl.BlockSpec((1, D_MIX, TM), col),
    ]
    params = pltpu.CompilerParams(dimension_semantics=("parallel", "arbitrary"), vmem_limit_bytes=VMEM_LIMIT)
    qaT, ka, vaT, iqT, ik, iwT, qbT, kcat, vbT, szT = pl.pallas_call(
        _proj_kernel,
        grid=grid_m,
        in_specs=[pl.BlockSpec((1, TM, D_MODEL), tok), pl.BlockSpec((1, 1, TM), col)]
        + [_full(w.shape) for w in weights],
        out_specs=proj_out_specs,
        out_shape=proj_out_shapes,
        compiler_params=params,
        name="proj",
    )(x, posr, *weights)

    grid_q = (B, S // TQ)
    oaT = pl.pallas_call(
        functools.partial(_attn_a_kernel, ktop=ktop),
        grid=grid_q,
        in_specs=[
            pl.BlockSpec((1, IDX_H * IDX_D, TQ), col),
            pl.BlockSpec((1, IDX_H, TQ), col),
            pl.BlockSpec((1, S, IDX_D), lambda b, i: (b, 0, 0)),
            pl.BlockSpec((1, D_A, TQ), col),
            pl.BlockSpec((1, KV_A, S, HD_A), lambda b, i: (b, 0, 0, 0)),
            pl.BlockSpec((1, n_kt, KV_A * V_SLAB, TK), lambda b, i: (b, 0, 0, 0)),
            pl.BlockSpec((1, 1, TQ), col),
            pl.BlockSpec((1, S, 1), lambda b, i: (b, 0, 0)),
        ],
        out_specs=pl.BlockSpec((1, D_A, TQ), col),
        out_shape=jax.ShapeDtypeStruct((B, D_A, S), f32),
        scratch_shapes=[
            pltpu.VMEM((S, TQ), f32),
            pltpu.VMEM((TK, TQ), f32),
            pltpu.VMEM((SUBLANES, TQ), f32),
            pltpu.VMEM((H_A, TK, TQ), f32),
            pltpu.VMEM((H_A, TQ), f32),
            pltpu.VMEM((H_A, TK, TQ), bf),
            pltpu.VMEM((H_A, TQ), f32),
            pltpu.VMEM((H_A, V_SLAB, TQ), f32),
            pltpu.SMEM((1,), jnp.int32),
        ],
        compiler_params=params,
        name="attn_a",
    )(iqT, iwT, ik, qaT, ka, vaT, posr, posc)

    obT = pl.pallas_call(
        _attn_b_kernel,
        grid=grid_q,
        in_specs=[
            pl.BlockSpec((1, H_B * QB_PAD, TQ), col),
            pl.BlockSpec((1, H_B, S, QB_PAD), lambda b, i: (b, 0, 0, 0)),
            pl.BlockSpec((1, n_kt, H_B * V_SLAB, TK), lambda b, i: (b, 0, 0, 0)),
        ],
        out_specs=pl.BlockSpec((1, D_B, TQ), col),
        out_shape=jax.ShapeDtypeStruct((B, D_B, S), f32),
        scratch_shapes=[
            pltpu.VMEM((H_B, TK, TQ), f32),
            pltpu.VMEM((H_B, TQ), f32),
            pltpu.VMEM((H_B, TK, TQ), bf),
            pltpu.VMEM((H_B, TQ), f32),
            pltpu.VMEM((H_B, V_SLAB, TQ), f32),
        ],
        compiler_params=params,
        name="attn_b",
    )(qbT, kcat, vbT)

    return pl.pallas_call(
        functools.partial(_out_kernel, alpha=alpha),
        grid=grid_m,
        in_specs=[
            pl.BlockSpec((1, D_A, TM), col),
            pl.BlockSpec((1, D_B, TM), col),
            pl.BlockSpec((1, D_MIX, TM), col),
            pl.BlockSpec((1, TM, D_MODEL), tok),
            _full((D_MIX, D_MODEL)),
            _full((D_A, 1)),
            _full((D_B, 1)),
            _full((1, D_MODEL)),
            _full((1, D_MODEL)),
        ],
        out_specs=pl.BlockSpec((1, TM, D_MODEL), tok),
        out_shape=jax.ShapeDtypeStruct((B, S, D_MODEL), f32),
        compiler_params=params,
        name="out",
    )(oaT, obT, szT, x, w_out.astype(bf), g_out_a.reshape(D_A, 1), g_out_b.reshape(D_B, 1),
      ln_g.reshape(1, D_MODEL), ln_b.reshape(1, D_MODEL))


def kernel(x, positions, w_in, g_q_lat, w_uq, g_kv_lat, w_ukv, g_out_a, g_out_b, w_out, ln_g, ln_b):
    depth = w_in.shape[0]
    alpha = (2.0 * depth) ** 0.25
    posf = positions.astype(jnp.float32)
    posr, posc = posf[:, None, :], posf[:, :, None]
    h = x
    for l in range(depth):
        h = _layer(h, posr, posc, w_in[l], g_q_lat[l], w_uq[l], g_kv_lat[l], w_ukv[l],
                   g_out_a[l], g_out_b[l], w_out[l], ln_g[l], ln_b[l], alpha)
    return h
```

```python
import functools

import numpy as np
import jax
import jax.numpy as jnp
from jax import lax
from jax.experimental import pallas as pl
from jax.experimental.pallas import tpu as pltpu

D_MODEL = 1024
H_A, KV_A, HD_A = 8, 2, 64
REP_A = H_A // KV_A
D_A = H_A * HD_A
IDX_H, IDX_D = 8, 64
TOPK_MAX = 256
H_B, D_NOPE, D_ROPE, V_B = 8, 64, 32, 64
Q_LORA, KV_LORA = 384, 256
D_B = H_B * V_B
D_MIX = D_A + D_B
ROPE_THETA = 10000.0
NEG_INF = -1e30
IN_SIZES = (D_A, KV_A * HD_A, KV_A * HD_A, IDX_H * IDX_D, IDX_D, IDX_H, Q_LORA, KV_LORA, D_ROPE, D_MIX)
IW_SCALE = IDX_H ** -0.5 * IDX_D ** -0.5
LOG2E = 1.4426950408889634
SCALE_B = (D_NOPE + D_ROPE) ** -0.5 * LOG2E
LANES, SUBLANES = 128, 8
BF16_ROWS = 2 * SUBLANES
QB_PAD = LANES
BIG = 3e38
SLOPES = tuple(2.0 ** (-8.0 * (h + 1) / H_A) for h in range(H_A))

V_ROWS = 64
V_SLAB = V_ROWS + BF16_ROWS

TM = 1024
TQ = 512
TK = 512
CNT_ROWS = 128
CHAINS = 4
BISECT_FIXED = 22
BISECT_CAP = 320
VMEM_LIMIT = 56 * 1024 * 1024

_NT = (((1,), (1,)), ((), ()))
_TN = (((0,), (0,)), ((), ()))


def _rms_rows(v, g, eps=1e-6):
    return v * lax.rsqrt(jnp.mean(v * v, axis=-1, keepdims=True) + eps) * g


def _proj_kernel(x_ref, posr_ref, wqaT, wiqT, wzT, wvaT, wiwT, wka, wik, wcq, wckv, wkrp, wkrr,
                 gq, wuqT, gkv, wukp, wuvT, invc,
                 qaT_o, ka_o, vaT_o, iqT_o, ik_o, iwT_o, qbT_o, kcat_o, vbT_o, szT_o):
    f32, bf = jnp.float32, jnp.bfloat16
    xb = x_ref[0].astype(bf)

    def nt(w):
        return lax.dot_general(w, xb, _NT, preferred_element_type=f32)

    def nn(w):
        return jnp.dot(xb, w, preferred_element_type=f32)

    qaT_o[0] = (nt(wqaT[...]) * LOG2E).astype(bf)
    iqT_o[0] = nt(wiqT[...]).astype(bf)
    iwT_o[0] = nt(wiwT[...]) * IW_SCALE
    ones_rows = jnp.where(lax.broadcasted_iota(jnp.int32, (V_SLAB - V_ROWS, TK), 0) == 0, 1.0, 0.0).astype(bf)
    va = nt(wvaT[...]).astype(bf)
    for c in range(TM // TK):
        for g in range(KV_A):
            vaT_o[0, c, g * V_SLAB:g * V_SLAB + V_ROWS] = va[g * HD_A:(g + 1) * HD_A, c * TK:(c + 1) * TK]
            vaT_o[0, c, g * V_SLAB + V_ROWS:(g + 1) * V_SLAB] = ones_rows
    z = nt(wzT[...])
    szT_o[0] = (z * jax.nn.sigmoid(z)).astype(bf)
    kk = nn(wka[...])
    for g in range(KV_A):
        ka_o[0, g] = kk[:, g * HD_A:(g + 1) * HD_A].astype(bf)
    ik_o[0] = nn(wik[...]).astype(bf)

    cqn = _rms_rows(nn(wcq[...]), gq[...]).astype(bf)
    qb = lax.dot_general(wuqT[...], cqn, _NT, preferred_element_type=f32)
    ang_t = invc[...] * posr_ref[0]
    cos_t, sin_t = jnp.cos(ang_t), jnp.sin(ang_t)
    half = D_ROPE // 2
    for h in range(H_B):
        b, o = h * (D_NOPE + D_ROPE), h * QB_PAD
        qbT_o[0, o:o + D_NOPE] = (qb[b:b + D_NOPE] * SCALE_B).astype(bf)
        x1 = qb[b + D_NOPE:b + D_NOPE + half]
        x2 = qb[b + D_NOPE + half:b + D_NOPE + D_ROPE]
        qbT_o[0, o + D_NOPE:o + D_NOPE + half] = ((x1 * cos_t - x2 * sin_t) * SCALE_B).astype(bf)
        qbT_o[0, o + D_NOPE + half:o + D_NOPE + D_ROPE] = ((x1 * sin_t + x2 * cos_t) * SCALE_B).astype(bf)
        qbT_o[0, o + D_NOPE + D_ROPE:o + QB_PAD] = jnp.zeros((QB_PAD - D_NOPE - D_ROPE, TM), bf)

    ckvn = _rms_rows(nn(wckv[...]), gkv[...]).astype(bf)
    knp = jnp.dot(ckvn, wukp[...], preferred_element_type=f32)
    cos_c = jnp.concatenate([cos_t, cos_t], axis=0).T
    sin_c = jnp.concatenate([sin_t, sin_t], axis=0).T
    kr_roped = (nn(wkrp[...]) * cos_c + nn(wkrr[...]) * sin_c).astype(bf)
    place = (lax.broadcasted_iota(jnp.int32, (D_ROPE, QB_PAD), 1)
             == lax.broadcasted_iota(jnp.int32, (D_ROPE, QB_PAD), 0) + D_NOPE)
    kro = jnp.dot(kr_roped, jnp.where(place, 1.0, 0.0).astype(bf), preferred_element_type=f32)
    for h in range(H_B):
        kcat_o[0, h] = (knp[:, h * QB_PAD:(h + 1) * QB_PAD] + kro).astype(bf)
    vb = lax.dot_general(wuvT[...], ckvn, _NT, preferred_element_type=f32).astype(bf)
    for c in range(TM // TK):
        for h in range(H_B):
            vbT_o[0, c, h * V_SLAB:h * V_SLAB + V_ROWS] = vb[h * V_B:(h + 1) * V_B, c * TK:(c + 1) * TK]
            vbT_o[0, c, h * V_SLAB + V_ROWS:(h + 1) * V_SLAB] = ones_rows


def _qk_stage(nheads, score_fn, s_scr, tmax_scr):
    for h in range(nheads):
        s = score_fn(h)
        s_scr[h] = s
        tmax_scr[h:h + 1, :] = jnp.max(s, axis=0, keepdims=True)


def _softmax_stage(nheads, s_scr, tmax_scr, p_scr, m_scr):
    alphas = []
    for h in range(nheads):
        m_old = m_scr[h:h + 1, :]
        m_new = jnp.maximum(m_old, tmax_scr[h:h + 1, :])
        p_scr[h] = jnp.exp2(s_scr[h] - m_new).astype(jnp.bfloat16)
        alphas.append(jnp.exp2(m_old - m_new))
        m_scr[h:h + 1, :] = m_new
    return alphas


def _pv_stage(nheads, v_fn, alphas, p_scr, acc_scr):
    for h in range(nheads):
        acc_scr[h] = alphas[h] * acc_scr[h] + jnp.dot(v_fn(h), p_scr[h], preferred_element_type=jnp.float32)


def _finish(nheads, o_ref, acc_scr):
    for h in range(nheads):
        o_ref[0, h * V_ROWS:(h + 1) * V_ROWS, :] = acc_scr[h, :V_ROWS, :] / acc_scr[h, V_ROWS:V_ROWS + 1, :]


def _attn_a_kernel(iqT_ref, iwT_ref, ik_ref, qaT_ref, ka_ref, vaT_ref, posr_ref, posc_ref, o_ref,
                   isc_scr, bias_scr, st_scr, s_scr, tmax_scr, p_scr, m_scr, acc_scr, flag_smem, *, ktop):
    f32 = jnp.float32
    qi = pl.program_id(1)
    t_idx = qi * TQ + lax.broadcasted_iota(jnp.int32, (1, TQ), 1)
    iw = iwT_ref[0]

    def rows_of(j):
        return pl.ds(pl.multiple_of(j * TK, TK), TK)

    def idx_tile(j):
        k = ik_ref[0, rows_of(j), :]
        tot = jnp.zeros((TK, TQ), f32)
        for h in range(IDX_H):
            rel = jnp.dot(k, iqT_ref[0, h * IDX_D:(h + 1) * IDX_D, :], preferred_element_type=f32)
            tot = tot + jnp.maximum(rel, 0.0) * iw[h:h + 1, :]
        return tot

    def p1(j, amax):
        tot = idx_tile(j)
        isc_scr[rows_of(j), :] = tot
        return jnp.maximum(amax, jnp.max(jnp.abs(tot), axis=0, keepdims=True))

    amax = lax.fori_loop(0, qi // 2, lambda c, a: p1(2 * c + 1, p1(2 * c, a)), jnp.zeros((1, TQ), f32))
    st_scr[6:7, :] = amax

    @pl.when(lax.rem(qi, 2) == 1)
    def _():
        st_scr[6:7, :] = p1(qi - 1, st_scr[6:7, :])

    amax = st_scr[6:7, :]
    tot = idx_tile(qi)
    s_idx = qi * TK + lax.broadcasted_iota(jnp.int32, (TK, TQ), 0)
    causal = s_idx <= t_idx
    isc_scr[rows_of(qi), :] = jnp.where(causal, tot, NEG_INF)
    amax = jnp.maximum(amax, jnp.max(jnp.where(causal, jnp.abs(tot), 0.0), axis=0, keepdims=True))

    want = jnp.minimum(t_idx + 1, ktop).astype(f32)
    n_tiles = qi + 1

    def count_ge(mid):
        def body(c, acc):
            x = isc_scr[pl.ds(pl.multiple_of(c * CNT_ROWS, CNT_ROWS), CNT_ROWS), :]
            ind = jnp.where(x >= mid, 1.0, 0.0)
            return acc + ind.reshape(CHAINS, CNT_ROWS // (SUBLANES * CHAINS), SUBLANES, TQ).sum(axis=1)
        acc = lax.fori_loop(0, n_tiles * (TK // CNT_ROWS), body, jnp.zeros((CHAINS, SUBLANES, TQ), f32))
        return jnp.sum(acc.reshape(CHAINS * SUBLANES, TQ), axis=0, keepdims=True)

    def bisect(_, st):
        lo, hi, clo, chi = st
        mid = 0.5 * (lo + hi)
        c = count_ge(mid)
        ge = c >= want
        return (jnp.where(ge, mid, lo), jnp.where(ge, hi, mid), jnp.where(ge, c, clo), jnp.where(ge, chi, c))

    st0 = (-amax, amax * 1.0001 + 1e-30, (t_idx + 1).astype(f32), jnp.zeros((1, TQ), f32))
    lo, hi, clo, chi = lax.fori_loop(0, BISECT_FIXED, bisect, st0)
    st_scr[0:1, :] = lo
    st_scr[1:2, :] = hi
    st_scr[2:3, :] = clo
    st_scr[3:4, :] = chi

    def unresolved_any(clo, chi, distinct):
        u = jnp.where((clo - chi) > (want - chi), distinct, 0.0)
        return (jnp.max(u) > 0.5).astype(jnp.int32)

    def refine_cond(c):
        it, more = c
        return jnp.logical_and(more > 0, it < BISECT_CAP)

    def refine(c):
        it, _ = c
        lo, hi, clo, chi = st_scr[0:1, :], st_scr[1:2, :], st_scr[2:3, :], st_scr[3:4, :]
        mid = 0.5 * (lo + hi)

        def body(j, carry):
            a8, mn8, mx8 = carry
            x = isc_scr[rows_of(j), :]
            ind = jnp.where(x >= mid, 1.0, 0.0)
            mn = jnp.where(x >= lo, x, BIG).reshape(TK // SUBLANES, SUBLANES, TQ).min(axis=0)
            mx = jnp.where(x < hi, x, -BIG).reshape(TK // SUBLANES, SUBLANES, TQ).max(axis=0)
            return (a8 + ind.reshape(TK // SUBLANES, SUBLANES, TQ).sum(axis=0), jnp.minimum(mn8, mn), jnp.maximum(mx8, mx))

        a8, mn8, mx8 = lax.fori_loop(
            0, n_tiles, body,
            (jnp.zeros((SUBLANES, TQ), f32), jnp.full((SUBLANES, TQ), BIG, f32), jnp.full((SUBLANES, TQ), -BIG, f32)))
        cnt = jnp.sum(a8, axis=0, keepdims=True)
        mn = jnp.min(mn8, axis=0, keepdims=True)
        mx = jnp.max(mx8, axis=0, keepdims=True)
        ge = cnt >= want
        lo2, hi2 = jnp.where(ge, mid, lo), jnp.where(ge, hi, mid)
        clo2, chi2 = jnp.where(ge, cnt, clo), jnp.where(ge, chi, cnt)
        st_scr[0:1, :] = lo2
        st_scr[1:2, :] = hi2
        st_scr[2:3, :] = clo2
        st_scr[3:4, :] = chi2
        return it + 1, unresolved_any(clo2, chi2, jnp.where(mx > mn, 1.0, 0.0))

    lax.while_loop(refine_cond, refine,
                   (jnp.int32(BISECT_FIXED), unresolved_any(clo, chi, jnp.ones((1, TQ), f32))))
    lo, hi, clo, chi = st_scr[0:1, :], st_scr[1:2, :], st_scr[2:3, :], st_scr[3:4, :]
    need = want - chi
    flag_smem[0] = unresolved_any(clo, chi, jnp.ones((1, TQ), f32))
    st_scr[4:5, :] = need
    st_scr[5:6, :] = jnp.zeros((1, TQ), f32)

    m_scr[...] = jnp.full(m_scr.shape, -BIG, f32)
    acc_scr[...] = jnp.zeros(acc_scr.shape, f32)
    pq = posr_ref[0]

    def select_bias(j):
        rows = rows_of(j)
        x = isc_scr[rows, :]
        nad = -jnp.abs(pq - posc_ref[0, rows, :])
        lo = st_scr[0:1, :]

        @pl.when(flag_smem[0] == 0)
        def _():
            bias_scr[...] = jnp.where(x >= lo, nad, NEG_INF)

        @pl.when(flag_smem[0] != 0)
        def _():
            hi, need, seen = st_scr[1:2, :], st_scr[4:5, :], st_scr[5:6, :]
            above = jnp.where(x >= hi, 1.0, 0.0)
            tied = jnp.where(x >= lo, 1.0, 0.0) - above
            r_i = lax.broadcasted_iota(jnp.int32, (TK, TK), 0)
            c_i = lax.broadcasted_iota(jnp.int32, (TK, TK), 1)
            tri = jnp.where(r_i >= c_i, 1.0, 0.0).astype(jnp.bfloat16)
            incl = jnp.dot(tri, tied.astype(jnp.bfloat16), preferred_element_type=f32)
            rank = incl - tied + seen
            keep = above + tied * jnp.where(rank < need, 1.0, 0.0)
            bias_scr[...] = jnp.where(keep > 0.5, nad, NEG_INF)
            st_scr[5:6, :] = seen + incl[TK - 1:TK, :]

    def qk(j):
        rows = rows_of(j)

        def score(h):
            qk_h = jnp.dot(ka_ref[0, h // REP_A, rows, :], qaT_ref[0, h * HD_A:(h + 1) * HD_A, :],
                           preferred_element_type=f32)
            return qk_h + (SLOPES[h] * LOG2E) * bias_scr[...]

        _qk_stage(H_A, score, s_scr, tmax_scr)

    def softmax():
        return _softmax_stage(H_A, s_scr, tmax_scr, p_scr, m_scr)

    def pv(j, alphas):
        _pv_stage(H_A, lambda h: vaT_ref[0, j, (h // REP_A) * V_SLAB:(h // REP_A + 1) * V_SLAB, :],
                  alphas, p_scr, acc_scr)

    select_bias(0)
    qk(0)

    def p3(j, _):
        select_bias(j + 1)
        alphas = softmax()
        qk(j + 1)
        pv(j, alphas)
        return 0

    lax.fori_loop(0, qi, p3, 0)
    pv(qi, softmax())
    _finish(H_A, o_ref, acc_scr)


def _attn_b_kernel(qbT_ref, kcat_ref, vbT_ref, o_ref, s_scr, tmax_scr, p_scr, m_scr, acc_scr):
    f32 = jnp.float32
    qi = pl.program_id(1)
    m_scr[...] = jnp.full(m_scr.shape, -BIG, f32)
    acc_scr[...] = jnp.zeros(acc_scr.shape, f32)

    def qk(j, masked):
        rows = pl.ds(pl.multiple_of(j * TK, TK), TK)
        if masked:
            s_idx = j * TK + lax.broadcasted_iota(jnp.int32, (TK, TQ), 0)
            t_idx = qi * TQ + lax.broadcasted_iota(jnp.int32, (TK, TQ), 1)
            causal = s_idx <= t_idx

        def score(h):
            s = jnp.dot(kcat_ref[0, h, rows, :], qbT_ref[0, h * QB_PAD:(h + 1) * QB_PAD, :],
                        preferred_element_type=f32)
            return jnp.where(causal, s, NEG_INF) if masked else s

        _qk_stage(H_B, score, s_scr, tmax_scr)

    def softmax():
        return _softmax_stage(H_B, s_scr, tmax_scr, p_scr, m_scr)

    def pv(j, alphas):
        _pv_stage(H_B, lambda h: vbT_ref[0, j, h * V_SLAB:(h + 1) * V_SLAB, :], alphas, p_scr, acc_scr)

    qk(qi, True)

    def body(i, _):
        alphas = softmax()
        qk(i, False)
        pv(jnp.where(i == 0, qi, i - 1), alphas)
        return 0

    lax.fori_loop(0, qi, body, 0)
    pv(jnp.maximum(qi - 1, 0), softmax())
    _finish(H_B, o_ref, acc_scr)


def _out_kernel(oaT_ref, obT_ref, szT_ref, x_ref, wout_ref, ga_ref, gb_ref, lng_ref, lnb_ref, o_ref, *, alpha):
    f32 = jnp.float32

    def rms_cols(v, g, eps=1e-6):
        return v * lax.rsqrt(jnp.mean(v * v, axis=0, keepdims=True) + eps) * g

    y_t = jnp.concatenate([rms_cols(oaT_ref[0], ga_ref[...]), rms_cols(obT_ref[0], gb_ref[...])], axis=0)
    y_t = (y_t * szT_ref[0].astype(f32)).astype(jnp.bfloat16)
    out = lax.dot_general(y_t, wout_ref[...], _TN, preferred_element_type=f32)
    hres = alpha * x_ref[0] + out
    mu = jnp.mean(hres, axis=-1, keepdims=True)
    d = hres - mu
    var = jnp.mean(d * d, axis=-1, keepdims=True)
    o_ref[0] = d * lax.rsqrt(var + 1e-5) * lng_ref[...] + lnb_ref[...]


def _full(shape):
    return pl.BlockSpec(shape, lambda *_: (0,) * len(shape))


def _layer(x, posr, posc, w_in, g_q_lat, w_uq, g_kv_lat, w_ukv, g_out_a, g_out_b, w_out, ln_g, ln_b, alpha):
    B, S, _ = x.shape
    f32, bf = jnp.float32, jnp.bfloat16
    assert S % TM == 0 and TM % TK == 0 and TQ == TK and HD_A == V_ROWS and V_B == V_ROWS
    ktop = min(TOPK_MAX, S // 4)
    n_kt = S // TK

    offs = np.concatenate([[0], np.cumsum(IN_SIZES)])
    wqa, wka, wva, wiq, wik, wiw, wcq, wckv, wkr, wz = [w_in[:, offs[i]:offs[i + 1]] for i in range(10)]
    half = D_ROPE // 2
    wkrp = wkr
    wkrr = jnp.concatenate([-wkr[:, half:], wkr[:, :half]], axis=1)
    wukv3 = w_ukv.reshape(KV_LORA, H_B, D_NOPE + V_B)
    wukp = jnp.zeros((KV_LORA, H_B, QB_PAD), f32).at[:, :, :D_NOPE].set(wukv3[:, :, :D_NOPE]).reshape(KV_LORA, H_B * QB_PAD)
    wuvT = wukv3[:, :, D_NOPE:].reshape(KV_LORA, D_B).T
    inv_freq = ROPE_THETA ** (-jnp.arange(0, D_ROPE, 2, dtype=f32) / D_ROPE)
    invc = inv_freq.reshape(half, 1)
    weights = [
        (wqa * HD_A ** -0.5).T.astype(bf), wiq.T.astype(bf), wz.T.astype(bf), wva.T.astype(bf), wiw.T.astype(bf),
        wka.astype(bf), wik.astype(bf), wcq.astype(bf), wckv.astype(bf), wkrp.astype(bf), wkrr.astype(bf),
        g_q_lat.reshape(1, Q_LORA), w_uq.T.astype(bf), g_kv_lat.reshape(1, KV_LORA), wukp.astype(bf), wuvT.astype(bf),
        invc,
    ]

    grid_m = (B, S // TM)
    tok = lambda b, i: (b, i, 0)
    col = lambda b, i: (b, 0, i)
    proj_out_shapes = [
        jax.ShapeDtypeStruct((B, D_A, S), bf),
        jax.ShapeDtypeStruct((B, KV_A, S, HD_A), bf),
        jax.ShapeDtypeStruct((B, n_kt, KV_A * V_SLAB, TK), bf),
        jax.ShapeDtypeStruct((B, IDX_H * IDX_D, S), bf),
        jax.ShapeDtypeStruct((B, S, IDX_D), bf),
        jax.ShapeDtypeStruct((B, IDX_H, S), f32),
        jax.ShapeDtypeStruct((B, H_B * QB_PAD, S), bf),
        jax.ShapeDtypeStruct((B, H_B, S, QB_PAD), bf),
        jax.ShapeDtypeStruct((B, n_kt, H_B * V_SLAB, TK), bf),
        jax.ShapeDtypeStruct((B, D_MIX, S), bf),
    ]
    proj_out_specs = [
        pl.BlockSpec((1, D_A, TM), col),
        pl.BlockSpec((1, KV_A, TM, HD_A), lambda b, i: (b, 0, i, 0)),
        pl.BlockSpec((1, TM // TK, KV_A * V_SLAB, TK), lambda b, i: (b, i, 0, 0)),
        pl.BlockSpec((1, IDX_H * IDX_D, TM), col),
        pl.BlockSpec((1, TM, IDX_D), tok),
        pl.BlockSpec((1, IDX_H, TM), col),
        pl.BlockSpec((1, H_B * QB_PAD, TM), col),
        pl.BlockSpec((1, H_B, TM, QB_PAD), lambda b, i: (b, 0, i, 0)),
        pl.BlockSpec((1, TM // TK, H_B * V_SLAB, TK), lambda b, i: (b, i, 0, 0)),
        pl.BlockSpec((1, D_MIX, TM), col),
    ]
    params = pltpu.CompilerParams(dimension_semantics=("parallel", "arbitrary"), vmem_limit_bytes=VMEM_LIMIT)
    qaT, ka, vaT, iqT, ik, iwT, qbT, kcat, vbT, szT = pl.pallas_call(
        _proj_kernel,
        grid=grid_m,
        in_specs=[pl.BlockSpec((1, TM, D_MODEL), tok), pl.BlockSpec((1, 1, TM), col)]
        + [_full(w.shape) for w in weights],
        out_specs=proj_out_specs,
        out_shape=proj_out_shapes,
        compiler_params=params,
        name="proj",
    )(x, posr, *weights)

    grid_q = (B, S // TQ)
    oaT = pl.pallas_call(
        functools.partial(_attn_a_kernel, ktop=ktop),
        grid=grid_q,
        in_specs=[
            pl.BlockSpec((1, IDX_H * IDX_D, TQ), col),
            pl.BlockSpec((1, IDX_H, TQ), col),
            pl.BlockSpec((1, S, IDX_D), lambda b, i: (b, 0, 0)),
            pl.BlockSpec((1, D_A, TQ), col),
            pl.BlockSpec((1, KV_A, S, HD_A), lambda b, i: (b, 0, 0, 0)),
            pl.BlockSpec((1, n_kt, KV_A * V_SLAB, TK), lambda b, i: (b, 0, 0, 0)),
            pl.BlockSpec((1, 1, TQ), col),
            pl.BlockSpec((1, S, 1), lambda b, i: (b, 0, 0)),
        ],
        out_specs=pl.BlockSpec((1, D_A, TQ), col),
        out_shape=jax.ShapeDtypeStruct((B, D_A, S), f32),
        scratch_shapes=[
            pltpu.VMEM((S, TQ), f32),
            pltpu.VMEM((TK, TQ), f32),
            pltpu.VMEM((SUBLANES, TQ), f32),
            pltpu.VMEM((H_A, TK, TQ), f32),
            pltpu.VMEM((H_A, TQ), f32),
            pltpu.VMEM((H_A, TK, TQ), bf),
            pltpu.VMEM((H_A, TQ), f32),
            pltpu.VMEM((H_A, V_SLAB, TQ), f32),
            pltpu.SMEM((1,), jnp.int32),
        ],
        compiler_params=params,
        name="attn_a",
    )(iqT, iwT, ik, qaT, ka, vaT, posr, posc)

    obT = pl.pallas_call(
        _attn_b_kernel,
        grid=grid_q,
        in_specs=[
            pl.BlockSpec((1, H_B * QB_PAD, TQ), col),
            pl.BlockSpec((1, H_B, S, QB_PAD), lambda b, i: (b, 0, 0, 0)),
            pl.BlockSpec((1, n_kt, H_B * V_SLAB, TK), lambda b, i: (b, 0, 0, 0)),
        ],
        out_specs=pl.BlockSpec((1, D_B, TQ), col),
        out_shape=jax.ShapeDtypeStruct((B, D_B, S), f32),
        scratch_shapes=[
            pltpu.VMEM((H_B, TK, TQ), f32),
            pltpu.VMEM((H_B, TQ), f32),
            pltpu.VMEM((H_B, TK, TQ), bf),
            pltpu.VMEM((H_B, TQ), f32),
            pltpu.VMEM((H_B, V_SLAB, TQ), f32),
        ],
        compiler_params=params,
        name="attn_b",
    )(qbT, kcat, vbT)

    return pl.pallas_call(
        functools.partial(_out_kernel, alpha=alpha),
        grid=grid_m,
        in_specs=[
            pl.BlockSpec((1, D_A, TM), col),
            pl.BlockSpec((1, D_B, TM), col),
            pl.BlockSpec((1, D_MIX, TM), col),
            pl.BlockSpec((1, TM, D_MODEL), tok),
            _full((D_MIX, D_MODEL)),
            _full((D_A, 1)),
            _full((D_B, 1)),
            _full((1, D_MODEL)),
            _full((1, D_MODEL)),
        ],
        out_specs=pl.BlockSpec((1, TM, D_MODEL), tok),
        out_shape=jax.ShapeDtypeStruct((B, S, D_MODEL), f32),
        compiler_params=params,
        name="out",
    )(oaT, obT, szT, x, w_out.astype(bf), g_out_a.reshape(D_A, 1), g_out_b.reshape(D_B, 1),
      ln_g.reshape(1, D_MODEL), ln_b.reshape(1, D_MODEL))


def kernel(x, positions, w_in, g_q_lat, w_uq, g_kv_lat, w_ukv, g_out_a, g_out_b, w_out, ln_g, ln_b):
    depth = w_in.shape[0]
    alpha = (2.0 * depth) ** 0.25
    posf = positions.astype(jnp.float32)
    posr, posc = posf[:, None, :], posf[:, :, None]
    h = x
    for l in range(depth):
        h = _layer(h, posr, posc, w_in[l], g_q_lat[l], w_uq[l], g_kv_lat[l], w_ukv[l],
                   g_out_a[l], g_out_b[l], w_out[l], ln_g[l], ln_b[l], alpha)
    return h
```
